```python
import jax, jax.numpy as jnp
from jax import lax
import numpy as np

D_MODEL = 4096
BATCH = 2
SEQ = 4096
DEPTH = 2

GRID_W = 64
CTX_LEN = 256
D_LRU = D_MODEL // 2
LRU_HEADS = 16
LRU_BLOCK = D_LRU // LRU_HEADS
CONV_W = 4
LRU_C = 8.0
D_POOL = D_MODEL // 2
POOL_WINDOWS = (2, 4, 8, 16)
N_POOL = len(POOL_WINDOWS)
POOL_GROUP = D_POOL // N_POOL
D_IN_EVEN = 2 * D_LRU + D_POOL
MLSTM_HEADS = 8
MLSTM_DV = D_MODEL // MLSTM_HEADS
MLSTM_DQK = MLSTM_DV // 2
MLSTM_CHUNK = 64
D_IN_ODD = 2 * MLSTM_HEADS * MLSTM_DQK + 2 * D_MODEL + 4 * MLSTM_HEADS
N_EXPERTS = 16
D_EXPERT = D_MODEL // 4
CAPACITY_FACTOR = 2
N_EVEN = (DEPTH + 1) // 2
N_ODD = DEPTH // 2
DN_ALPHA = (2 * DEPTH) ** 0.25
DN_BETA = (8 * DEPTH) ** -0.25
LN_EPS = 1e-5

kernel_name = "hybrid_rglru_pool_mlstm_ecmoe_diffusion"

F32 = jnp.float32


def flip_seq(t):
    return t[:, ::-1]


def no_flip(t):
    return t


def layer_norm(x, g=None, b=None):
    xf = x.astype(F32)
    mu = jnp.mean(xf, axis=-1, keepdims=True)
    var = jnp.mean(jnp.square(xf - mu), axis=-1, keepdims=True)
    y = (xf - mu) * lax.rsqrt(var + LN_EPS)
    if g is not None:
        y = y * g + b
    return y.astype(x.dtype)


def adaln(cond, w, b):
    m = jax.nn.silu(cond) @ w + b
    return jnp.split(m[..., None, :], 6, axis=-1)


def dwconv(x, w, b):
    C = x.shape[-1]
    y = lax.conv_general_dilated(x, w[:, None, :].astype(x.dtype), window_strides=(1,),
                                 padding=[((CONV_W - 1) // 2, CONV_W // 2)],
                                 dimension_numbers=('NWC', 'WIO', 'NWC'), feature_group_count=C)
    return y + b


def linear_scan(a, bx, h0):
    def combine(l, r):
        return (l[0] * r[0], r[0] * l[1] + r[1])
    A, H = lax.associative_scan(combine, (a, bx), axis=1)
    return H + A * h0[:, None, :]


def rglru_coeffs(x, gate_w, gate_b, lam):
    B, L, _ = x.shape
    xb = x.reshape(B, L, LRU_HEADS, LRU_BLOCK)
    g = jnp.einsum('blhi,ghij->gblhj', xb, gate_w).reshape(2, B, L, D_LRU) + gate_b[:, None, None, :]
    r = jax.nn.sigmoid(g[0].astype(F32))
    i = jax.nn.sigmoid(g[1].astype(F32))
    log_a = LRU_C * r * jax.nn.log_sigmoid(lam.astype(F32))
    a = jnp.exp(log_a)
    mult = jnp.sqrt(-jnp.expm1(2.0 * log_a))
    return a, mult * (i * x.astype(F32))


def rglru_bidir(xc, xl, gate_w, gate_b, lam):
    B = xl.shape[0]
    out_c = 0.0
    out_l = 0.0
    for d in range(2):
        f = flip_seq if d else no_flip
        ac, bc = rglru_coeffs(f(xc), gate_w[d], gate_b[d], lam[d])
        hc = linear_scan(ac, bc, jnp.zeros((B, D_LRU), F32))
        al, bl = rglru_coeffs(f(xl), gate_w[d], gate_b[d], lam[d])
        hl = linear_scan(al, bl, hc[:, -1])
        out_c = out_c + f(hc)
        out_l = out_l + f(hl)
    return out_c, out_l


def window_mean(x, w, axis):
    L = x.shape[axis]
    pad = [(0, 0)] * x.ndim
    pad[axis] = (1, 0)
    cs = jnp.pad(jnp.cumsum(x.astype(F32), axis=axis), pad)
    pos = jnp.arange(L)
    lo = jnp.clip(pos - w // 2, 0, L)
    hi = jnp.clip(pos + w - w // 2, 0, L)
    shape = [1] * x.ndim
    shape[axis] = L
    cnt = (hi - lo).astype(F32).reshape(shape)
    return (jnp.take(cs, hi, axis=axis) - jnp.take(cs, lo, axis=axis)) / cnt


def pool_mix(s, pool_w, pool_scale, on_grid):
    B, L, _ = s.shape
    groups = s.reshape(B, L, N_POOL, POOL_GROUP)
    diffs = []
    for g, w in enumerate(POOL_WINDOWS):
        xg = groups[:, :, g]
        if on_grid:
            rows = L // GRID_W
            grid = xg.reshape(B, rows, GRID_W, POOL_GROUP)
            mean = window_mean(window_mean(grid, w, 2), w, 1).reshape(B, L, POOL_GROUP)
        else:
            mean = window_mean(xg, w, 1)
        diffs.append((mean - xg.astype(F32)).astype(s.dtype))
    d = jnp.stack(diffs, axis=2)
    y = jnp.einsum('blgi,gij->blgj', d, pool_w).reshape(B, L, D_POOL)
    return y * pool_scale


def even_mixer(uc, ul, w_in, conv_w, conv_b, gate_w, gate_b, lam, pool_w, pool_scale, w_out, ctx_out):
    xc, zc, sc = jnp.split(uc @ w_in, [D_LRU, 2 * D_LRU], axis=-1)
    xl, zl, sl = jnp.split(ul @ w_in, [D_LRU, 2 * D_LRU], axis=-1)
    hc, hl = rglru_bidir(dwconv(xc, conv_w, conv_b), dwconv(xl, conv_w, conv_b), gate_w, gate_b, lam)
    yl = jnp.concatenate([hl.astype(ul.dtype) * jax.nn.gelu(zl),
                          pool_mix(sl, pool_w, pool_scale, True)], axis=-1) @ w_out
    yc = None
    if ctx_out:
        yc = jnp.concatenate([hc.astype(uc.dtype) * jax.nn.gelu(zc),
                              pool_mix(sc, pool_w, pool_scale, False)], axis=-1) @ w_out
    return yc, yl


def mlstm_chunked(q, k, v, ig, lf, state):
    B, L, H, _ = q.shape
    nc = L // MLSTM_CHUNK

    def to_chunks(t):
        return jnp.moveaxis(t.reshape((B, nc, MLSTM_CHUNK) + t.shape[2:]), 1, 0)

    xs = (to_chunks(q), to_chunks(k), to_chunks(v), to_chunks(ig), to_chunks(lf))
    causal = jnp.tril(jnp.ones((MLSTM_CHUNK, MLSTM_CHUNK), bool))[None, :, :, None]

    def step(carry, inp):
        C, n, m = carry
        qc, kc, vc, ic, fc = inp
        b = jnp.cumsum(fc, axis=1)
        dmat = jnp.where(causal, b[:, :, None, :] - b[:, None, :, :] + ic[:, None, :, :], -jnp.inf)
        inter = b + m[:, None, :]
        m_comb = jnp.maximum(inter, jnp.max(dmat, axis=2))
        s = jnp.einsum('bthd,bshd->btsh', qc, kc) * jnp.exp(dmat - m_comb[:, :, None, :])
        e_inter = jnp.exp(inter - m_comb)
        num = jnp.einsum('btsh,bshv->bthv', s, vc) + e_inter[..., None] * jnp.einsum('bthd,bhdv->bthv', qc, C)
        den = jnp.sum(s, axis=2) + e_inter * jnp.einsum('bthd,bhd->bth', qc, n)
        h = num / jnp.maximum(jnp.abs(den), jnp.exp(-m_comb))[..., None]
        b_end = b[:, -1]
        g = b_end[:, None, :] - b + ic
        m_new = jnp.maximum(b_end + m, jnp.max(g, axis=1))
        w_s = jnp.exp(g - m_new[:, None, :])
        decay = jnp.exp(b_end + m - m_new)
        C_new = decay[..., None, None] * C + jnp.einsum('bsh,bshd,bshv->bhdv', w_s, kc, vc)
        n_new = decay[..., None] * n + jnp.einsum('bsh,bshd->bhd', w_s, kc)
        return (C_new, n_new, m_new), h

    state, hs = lax.scan(step, state, xs)
    return jnp.moveaxis(hs, 0, 1).reshape(B, L, H, MLSTM_DV), state


def head_norm(h, g):
    B, L = h.shape[:2]
    mu = jnp.mean(h, axis=-1, keepdims=True)
    var = jnp.mean(jnp.square(h - mu), axis=-1, keepdims=True)
    return ((h - mu) * lax.rsqrt(var + LN_EPS)).reshape(B, L, D_MODEL) * g


def odd_mixer(uc, ul, w_in, gate_b, norm_g, w_out, ctx_out):
    qk = MLSTM_HEADS * MLSTM_DQK
    split_at = [qk, 2 * qk, 2 * qk + D_MODEL, 2 * qk + 2 * D_MODEL]

    def project(u):
        B, L, _ = u.shape
        q, k, v, o, g = jnp.split(u @ w_in, split_at, axis=-1)
        q = q.reshape(B, L, MLSTM_HEADS, MLSTM_DQK).astype(F32) * (MLSTM_DQK ** -0.5)
        k = k.reshape(B, L, MLSTM_HEADS, MLSTM_DQK).astype(F32)
        v = v.reshape(B, L, MLSTM_HEADS, MLSTM_DV).astype(F32)
        g = (g + gate_b).astype(F32).reshape(B, L, 2, 2, MLSTM_HEADS)
        return q, k, v, o, g

    qc, kc, vc, oc, gc = project(uc)
    ql, kl, vl, ol, gl = project(ul)
    B = ul.shape[0]
    h_c = 0.0
    h_l = 0.0
    for d in range(2):
        f = flip_seq if d else no_flip
        state0 = (jnp.zeros((B, MLSTM_HEADS, MLSTM_DQK, MLSTM_DV), F32),
                  jnp.zeros((B, MLSTM_HEADS, MLSTM_DQK), F32),
                  jnp.zeros((B, MLSTM_HEADS), F32))
        out_c, state = mlstm_chunked(f(qc), f(kc), f(vc), f(gc[:, :, d, 0]),
                                     jax.nn.log_sigmoid(f(gc[:, :, d, 1])), state0)
        out_l, _ = mlstm_chunked(f(ql), f(kl), f(vl), f(gl[:, :, d, 0]),
                                 jax.nn.log_sigmoid(f(gl[:, :, d, 1])), state)
        h_c = h_c + f(out_c)
        h_l = h_l + f(out_l)

    def finish(h, o):
        y = jax.nn.sigmoid(o.astype(F32)) * head_norm(h, norm_g)
        return y.astype(o.dtype) @ w_out

    yl = finish(h_l, ol)
    yc = finish(h_c, oc) if ctx_out else None
    return yc, yl


def expert_choice_moe(x, w_router, w_gu, w_down):
    B, n, D = x.shape
    cap = CAPACITY_FACTOR * n // N_EXPERTS
    aff = jax.nn.softmax((x @ w_router).astype(F32), axis=-1)
    gsel, idx = lax.top_k(jnp.swapaxes(aff, 1, 2), cap)
    xe = jax.vmap(lambda xb, ib: xb[ib])(x, idx)
    hg, hu = jnp.split(jnp.einsum('becd,edf->becf', xe, w_gu), 2, axis=-1)
    ye = jnp.einsum('becf,efd->becd', jax.nn.silu(hg) * hu, w_down) * gsel[..., None].astype(x.dtype)
    return jax.vmap(lambda ib, yb: jnp.zeros((n, D), yb.dtype).at[ib.reshape(-1)].add(yb.reshape(-1, D)))(idx, ye)


def setup_inputs(seed: int = 0) -> dict:
    key = jax.random.key(seed)
    ks = jax.random.split(key, 26)
    D = D_MODEL

    def nrm(k, shape, s):
        return jax.random.normal(k, shape, F32) * s

    x = nrm(ks[0], (BATCH, SEQ, D), 1.0)
    c = nrm(ks[1], (BATCH, D), 1.0)
    ctx = nrm(ks[2], (BATCH, CTX_LEN, D), 1.0)
    c_ctx = nrm(ks[3], (D,), 1.0)
    ada_w = nrm(ks[4], (DEPTH, D, 6 * D), 0.5 * D ** -0.5)
    ada_b = nrm(ks[5], (DEPTH, 6 * D), 0.02)
    ln_g = 1.0 + nrm(ks[6], (DEPTH, 2, D), 0.02)
    ln_b = nrm(ks[7], (DEPTH, 2, D), 0.02)
    even_w_in = nrm(ks[8], (N_EVEN, D, D_IN_EVEN), D ** -0.5)
    even_conv_w = nrm(ks[9], (N_EVEN, CONV_W, D_LRU), CONV_W ** -0.5)
    even_conv_b = nrm(ks[10], (N_EVEN, D_LRU), 0.02)
    lru_gate_w = nrm(ks[11], (N_EVEN, 2, 2, LRU_HEADS, LRU_BLOCK, LRU_BLOCK), LRU_BLOCK ** -0.5)
    lru_gate_b = nrm(ks[12], (N_EVEN, 2, 2, D_LRU), 0.02)
    a_pow_c = jax.random.uniform(ks[13], (N_EVEN, 2, D_LRU), F32, 0.9, 0.999)
    sig = a_pow_c ** (1.0 / LRU_C)
    lru_lambda = jnp.log(sig) - jnp.log1p(-sig)
    pool_w = nrm(ks[14], (N_EVEN, N_POOL, POOL_GROUP, POOL_GROUP), POOL_GROUP ** -0.5)
    pool_scale = 1.0 + nrm(ks[15], (N_EVEN, D_POOL), 0.02)
    even_w_out = nrm(ks[16], (N_EVEN, D_LRU + D_POOL, D), DN_BETA * (D_LRU + D_POOL) ** -0.5)
    odd_w_in = nrm(ks[17], (N_ODD, D, D_IN_ODD), D ** -0.5)
    i_b = nrm(ks[18], (N_ODD, 2, 1, MLSTM_HEADS), 0.1)
    f_b = jnp.linspace(3.0, 6.0, MLSTM_HEADS, dtype=F32) + nrm(ks[19], (N_ODD, 2, 1, MLSTM_HEADS), 0.1)
    odd_gate_b = jnp.concatenate([i_b, f_b], axis=2).reshape(N_ODD, 4 * MLSTM_HEADS)
    odd_norm_g = 1.0 + nrm(ks[20], (N_ODD, D), 0.02)
    odd_w_out = nrm(ks[21], (N_ODD, D, D), DN_BETA * D ** -0.5)
    router_w = nrm(ks[22], (DEPTH, D, N_EXPERTS), D ** -0.5)
    expert_w_gu = nrm(ks[23], (DEPTH, N_EXPERTS, D, 2 * D_EXPERT), D ** -0.5)
    expert_w_down = nrm(ks[24], (DEPTH, N_EXPERTS, D_EXPERT, D), DN_BETA * D_EXPERT ** -0.5)
    return {"x": x, "c": c, "ctx": ctx, "c_ctx": c_ctx, "ada_w": ada_w, "ada_b": ada_b,
            "ln_g": ln_g, "ln_b": ln_b, "even_w_in": even_w_in, "even_conv_w": even_conv_w,
            "even_conv_b": even_conv_b, "lru_gate_w": lru_gate_w, "lru_gate_b": lru_gate_b,
            "lru_lambda": lru_lambda, "pool_w": pool_w, "pool_scale": pool_scale,
            "even_w_out": even_w_out, "odd_w_in": odd_w_in, "odd_gate_b": odd_gate_b,
            "odd_norm_g": odd_norm_g, "odd_w_out": odd_w_out, "router_w": router_w,
            "expert_w_gu": expert_w_gu, "expert_w_down": expert_w_down}


def reference(x, c, ctx, c_ctx, ada_w, ada_b, ln_g, ln_b, even_w_in, even_conv_w, even_conv_b,
              lru_gate_w, lru_gate_b, lru_lambda, pool_w, pool_scale, even_w_out, odd_w_in,
              odd_gate_b, odd_norm_g, odd_w_out, router_w, expert_w_gu, expert_w_down):
    h_lat = layer_norm(x)
    h_ctx = layer_norm(ctx)
    for layer in range(DEPTH):
        j = layer // 2
        need_ctx = layer < DEPTH - 1
        sh_l, sc_l, g_l, sh2_l, sc2_l, g2_l = adaln(c, ada_w[layer], ada_b[layer])
        sh_c, sc_c, g_c, sh2_c, sc2_c, g2_c = adaln(c_ctx, ada_w[layer], ada_b[layer])
        u_lat = h_lat * (1.0 + sc_l) + sh_l
        u_ctx = h_ctx * (1.0 + sc_c) + sh_c
        if layer % 2 == 0:
            y_ctx, y_lat = even_mixer(u_ctx, u_lat, even_w_in[j], even_conv_w[j], even_conv_b[j],
                                      lru_gate_w[j], lru_gate_b[j], lru_lambda[j], pool_w[j],
                                      pool_scale[j], even_w_out[j], need_ctx)
        else:
            y_ctx, y_lat = odd_mixer(u_ctx, u_lat, odd_w_in[j], odd_gate_b[j], odd_norm_g[j],
                                     odd_w_out[j], need_ctx)
        h_lat = layer_norm(DN_ALPHA * h_lat + g_l * y_lat, ln_g[layer, 0], ln_b[layer, 0])
        m_lat = expert_choice_moe(h_lat * (1.0 + sc2_l) + sh2_l, router_w[layer], expert_w_gu[layer], expert_w_down[layer])
        h_lat = layer_norm(DN_ALPHA * h_lat + g2_l * m_lat, ln_g[layer, 1], ln_b[layer, 1])
        if need_ctx:
            h_ctx = layer_norm(DN_ALPHA * h_ctx + g_c * y_ctx, ln_g[layer, 0], ln_b[layer, 0])
            m_ctx = expert_choice_moe(h_ctx * (1.0 + sc2_c) + sh2_c, router_w[layer], expert_w_gu[layer], expert_w_down[layer])
            h_ctx = layer_norm(DN_ALPHA * h_ctx + g2_c * m_ctx, ln_g[layer, 1], ln_b[layer, 1])
    return h_lat
```

```python
import functools

import jax
import jax.numpy as jnp
from jax import lax
from jax.experimental import pallas as pl
from jax.experimental.pallas import tpu as pltpu

F32 = jnp.float32
BF16 = jnp.bfloat16
I32 = jnp.int32

LN_EPS = 1e-5
TINY = 1e-30
GRID_W = 64
LRU_C = 8.0
POOL_WINDOWS = (2, 4, 8, 16)
CAPACITY_FACTOR = 2

LANE = 128
SUBLANE = 8
LN_ROWS = 64
ROW_TILE = 256
MM_TM = 512
MM_TN = 1024
MM_KC = 512
LRU_HEADS_PER_STEP = 2
SCAN_UNROLL = 8
MLSTM_T = 256
MOE_FC = 256
MOE_TN = 2048
COMB_W_SHIFT = 6
COMB_W = 1 << COMB_W_SHIFT
COMB_ALIGN_SHIFT = 4
COMB_ALIGN = 1 << COMB_ALIGN_SHIFT
VMEM_LIMIT = 56 * 1024 * 1024
VMEM_LIMIT_BIG = 60 * 1024 * 1024


def _cp(n_axes, vmem=VMEM_LIMIT):
    return pltpu.CompilerParams(dimension_semantics=("arbitrary",) * n_axes, vmem_limit_bytes=vmem)


def _ln_rows(v):
    mu = jnp.mean(v, axis=-1, keepdims=True)
    d = v - mu
    var = jnp.mean(d * d, axis=-1, keepdims=True)
    return d * lax.rsqrt(var + LN_EPS)


def _sigmoid(x):
    return 0.5 + 0.5 * jnp.tanh(0.5 * x)


def _stage_rows(scr, vecs):
    for k, v in enumerate(vecs):
        scr[k] = jnp.broadcast_to(v, scr.shape[1:])


def _row_blocks(n_rows, body):
    def step(i, carry):
        body(pl.multiple_of(i * LN_ROWS, LN_ROWS))
        return carry
    lax.fori_loop(0, n_rows // LN_ROWS, step, 0)


def _halves(r0):
    return [pl.ds(r0 + k * SUBLANE, SUBLANE) for k in range(LN_ROWS // SUBLANE)]


def _adaln_kernel(c_ref, w_ref, b_ref, o_ref):
    c = c_ref[...]
    s = (c * _sigmoid(c)).astype(BF16)
    o_ref[...] = jnp.dot(s, w_ref[...].astype(BF16), preferred_element_type=F32) + b_ref[...]


def _adaln(cond8, ada_w, ada_b):
    depth, D, N = ada_w.shape
    tn = 512
    return pl.pallas_call(
        _adaln_kernel,
        grid=(depth, N // tn),
        in_specs=[pl.BlockSpec((8, D), lambda l, j: (0, 0)),
                  pl.BlockSpec((None, D, tn), lambda l, j: (l, 0, j)),
                  pl.BlockSpec((None, 1, tn), lambda l, j: (l, 0, j))],
        out_specs=pl.BlockSpec((None, 8, tn), lambda l, j: (l, 0, j)),
        out_shape=jax.ShapeDtypeStruct((depth, 8, N), F32),
        compiler_params=_cp(2), name="adaln",
    )(cond8, ada_w, ada_b.reshape(depth, 1, N))


def _ln_in_kernel(xl_ref, xc_ref, mod_ref, h_ref, u_ref, inv, *, tpb):
    is_ctx = (pl.program_id(0) % tpb) == 0
    _stage_rows(inv, [1.0 + mod_ref[1:2, :], mod_ref[0:1, :]])

    def go(src):
        def blk(r0):
            us = []
            for rows in _halves(r0):
                h = _ln_rows(src[rows, :])
                h_ref[rows, :] = h
                us.append(h * inv[0] + inv[1])
            u_ref[pl.ds(r0, LN_ROWS), :] = jnp.concatenate(us, axis=0).astype(BF16)
        _row_blocks(ROW_TILE, blk)

    pl.when(is_ctx)(lambda: go(xc_ref))
    pl.when(jnp.logical_not(is_ctx))(lambda: go(xl_ref))


def _ln_in(xl, xc, mods, B, L, Lc):
    D = xl.shape[-1]
    tpb = (L + Lc) // ROW_TILE
    lt = L // ROW_TILE
    R = B * (L + Lc)
    return pl.pallas_call(
        functools.partial(_ln_in_kernel, tpb=tpb),
        grid=(B * tpb,),
        in_specs=[pl.BlockSpec((ROW_TILE, D), lambda i: ((i // tpb) * lt + jnp.maximum(i % tpb - 1, 0), 0)),
                  pl.BlockSpec((ROW_TILE, D), lambda i: (i // tpb, 0)),
                  pl.BlockSpec((None, None, 6, D), lambda i: (0, jnp.where(i % tpb == 0, B, i // tpb), 0, 0))],
        out_specs=[pl.BlockSpec((ROW_TILE, D), lambda i: (i, 0)),
                   pl.BlockSpec((ROW_TILE, D), lambda i: (i, 0))],
        out_shape=[jax.ShapeDtypeStruct((R, D), F32), jax.ShapeDtypeStruct((R, D), BF16)],
        scratch_shapes=[pltpu.VMEM((2, SUBLANE, D), F32)],
        compiler_params=_cp(1), name="ln_in",
    )(xl, xc, mods)


def _mm_kernel(*refs, ksplits, w_is_nk, piece):
    nx = len(ksplits)
    x_refs, w_ref, o_ref, wb_ref = refs[:nx], refs[nx], refs[nx + 1], refs[nx + 2]

    @pl.when(pl.program_id(1) == 0)
    def _():
        if w_is_nk:
            for k0 in range(0, wb_ref.shape[0], MM_KC):
                wb_ref[k0:k0 + MM_KC, :] = w_ref[:, k0:k0 + MM_KC].T.astype(BF16)
        else:
            wb_ref[...] = w_ref[...].astype(BF16)

    acc = None
    for x_ref, (k0, k1) in zip(x_refs, ksplits):
        p = jnp.dot(x_ref[...], wb_ref[k0:k1, :], preferred_element_type=F32)
        acc = p if acc is None else acc + p
    if piece is None:
        o_ref[...] = acc.astype(o_ref.dtype)
    else:
        for c in range(o_ref.shape[0]):
            o_ref[c] = acc[:, c * piece:(c + 1) * piece].astype(o_ref.dtype)


def _matmul(xs, w, widx, n_cols, out_dtype, w_is_nk=False, piece=None, tm=MM_TM, name="mm"):
    R = xs[0].shape[0]
    K = w.shape[2] if w_is_nk else w.shape[1]
    tn = next(t for t in (MM_TN, MM_TN // 2, MM_TN // 4, LANE) if n_cols % t == 0)
    ksplits, k0 = [], 0
    for x in xs:
        ksplits.append((k0, k0 + x.shape[1]))
        k0 += x.shape[1]
    assert k0 == K and R % tm == 0 and K % MM_KC == 0
    if w_is_nk:
        w_spec = pl.BlockSpec((None, tn, K), lambda j, i: (widx, j, 0))
    else:
        w_spec = pl.BlockSpec((None, K, tn), lambda j, i: (widx, 0, j))
    if piece is None:
        out_spec = pl.BlockSpec((tm, tn), lambda j, i: (i, j))
        out_shape = jax.ShapeDtypeStruct((R, n_cols), out_dtype)
    else:
        assert tn % piece == 0
        out_spec = pl.BlockSpec((tn // piece, tm, piece), lambda j, i: (j, i, 0))
        out_shape = jax.ShapeDtypeStruct((n_cols // piece, R, piece), out_dtype)
    return pl.pallas_call(
        functools.partial(_mm_kernel, ksplits=tuple(ksplits), w_is_nk=w_is_nk, piece=piece),
        grid=(n_cols // tn, R // tm),
        in_specs=[pl.BlockSpec((tm, x.shape[1]), lambda j, i: (i, 0)) for x in xs] + [w_spec],
        out_specs=out_spec,
        out_shape=out_shape,
        scratch_shapes=[pltpu.VMEM((K, tn), BF16)],
        compiler_params=_cp(2, VMEM_LIMIT_BIG), name=name,
    )(*xs, w)


def _scan_pitch(seg):
    return seg + SUBLANE


def _rglru_kernel(x_ref, z_ref, cw_ref, cb_ref, gw_ref, gb_ref, lam_ref, o_ref,
                  xs_scr, a_scr, b_scr, o_scr, *, Lc, L, nh):
    cw = cw_ref[...]
    cb = cb_ref[...]
    gb_half = 0.5 * gb_ref[...]
    gw_half = [[[(0.5 * gw_ref[d, g, hd]).astype(BF16) for hd in range(nh)] for g in range(2)] for d in range(2)]
    half_c_ls = (0.5 * LRU_C) * jax.nn.log_sigmoid(lam_ref[...])
    width = nh * LANE
    pad = SUBLANE
    off_c, off_l = pad, 2 * pad + Lc
    sequences = ((0, Lc, 0, off_c), (Lc, L, SUBLANE * _scan_pitch(Lc // SUBLANE), off_l))

    zpad = jnp.zeros((pad, width), F32)
    xs_scr[0:pad, :] = zpad
    xs_scr[off_c + Lc:off_c + Lc + pad, :] = zpad
    xs_scr[off_l + L:off_l + L + pad, :] = zpad
    for hd in range(nh):
        xs_scr[off_c:off_c + Lc, hd * LANE:(hd + 1) * LANE] = x_ref[hd, 0:Lc, :].astype(F32)
        xs_scr[off_l:off_l + L, hd * LANE:(hd + 1) * LANE] = x_ref[hd, Lc:Lc + L, :].astype(F32)

    for r0, n, base, off in sequences:
        seg = n // SUBLANE
        pitch = _scan_pitch(seg)
        for j in range(SUBLANE):
            s0 = off + j * seg
            xc = (cw[0:1] * xs_scr[s0 - 1:s0 - 1 + seg, :] + cw[1:2] * xs_scr[s0:s0 + seg, :]
                  + cw[2:3] * xs_scr[s0 + 1:s0 + 1 + seg, :] + cw[3:4] * xs_scr[s0 + 2:s0 + 2 + seg, :] + cb)
            xb = xc.astype(BF16)
            for hd in range(nh):
                cols = slice(hd * LANE, (hd + 1) * LANE)
                xh = xc[:, cols]
                xbh = xb[:, cols]
                for d in range(2):
                    tr = jnp.tanh(jnp.dot(xbh, gw_half[d][0][hd], preferred_element_type=F32) + gb_half[2 * d:2 * d + 1, cols])
                    ti = jnp.tanh(jnp.dot(xbh, gw_half[d][1][hd], preferred_element_type=F32) + gb_half[2 * d + 1:2 * d + 2, cols])
                    log_a = half_c_ls[d:d + 1, cols] * (1.0 + tr)
                    th = jnp.tanh(log_a)
                    p = -2.0 * th
                    mult = p * lax.rsqrt(jnp.maximum(p * (1.0 - th), TINY))
                    bx = mult * ((0.5 + 0.5 * ti) * xh)
                    rows = slice(base + j * pitch, base + j * pitch + seg)
                    a_scr[d * nh + hd, rows, :] = jnp.exp(log_a)
                    b_scr[d * nh + hd, rows, :] = bx

    rid = lax.broadcasted_iota(I32, (SUBLANE, LANE), 0)

    def carries(H, A, init, reverse):
        c = init
        cv = jnp.broadcast_to(init, (SUBLANE, LANE))
        order = range(SUBLANE - 1, -1, -1) if reverse else range(SUBLANE)
        for j in order:
            cv = jnp.where(rid == j, c, cv)
            c = A[j:j + 1] * c + H[j:j + 1]
        return cv, c

    zero = jnp.zeros((SUBLANE, LANE), F32)
    one = jnp.ones((SUBLANE, LANE), F32)
    state = [jnp.zeros((1, LANE), F32)] * (2 * nh)
    for r0, n, base, off in sequences:
        seg = n // SUBLANE
        pitch = _scan_pitch(seg)

        def rows_at(k, t, base=base, seg=seg, pitch=pitch):
            return pl.ds(base + (t if k < nh else seg - 1 - t), SUBLANE, stride=pitch)

        def pair(k, i):
            a0 = a_scr[k, rows_at(k, 2 * i), :]
            b0 = b_scr[k, rows_at(k, 2 * i), :]
            a1 = a_scr[k, rows_at(k, 2 * i + 1), :]
            return a0, b0, a1 * a0, a1 * b0 + b_scr[k, rows_at(k, 2 * i + 1), :]

        def ends(i, c):
            out = []
            for k in range(2 * nh):
                _, _, a2, b2 = pair(k, i)
                out += [a2 * c[2 * k] + b2, a2 * c[2 * k + 1]]
            return tuple(out)

        fin = lax.fori_loop(0, seg // 2, ends, (zero, one) * (2 * nh), unroll=SCAN_UNROLL)
        cvs = []
        for k in range(2 * nh):
            cv, state[k] = carries(fin[2 * k], fin[2 * k + 1], state[k], k >= nh)
            cvs.append(cv)

        def full(i, c):
            out = []
            for k in range(2 * nh):
                a0, b0, a2, b2 = pair(k, i)
                o_scr[k, rows_at(k, 2 * i), :] = a0 * c[k] + b0
                h = a2 * c[k] + b2
                o_scr[k, rows_at(k, 2 * i + 1), :] = h
                out.append(h)
            return tuple(out)

        lax.fori_loop(0, seg // 2, full, tuple(cvs), unroll=SCAN_UNROLL)

        for j in range(SUBLANE):
            rows = slice(base + j * pitch, base + j * pitch + seg)
            orow = slice(r0 + j * seg, r0 + (j + 1) * seg)
            for hd in range(nh):
                h = o_scr[hd, rows, :] + o_scr[nh + hd, rows, :]
                g = jax.nn.gelu(z_ref[hd, orow, :].astype(F32))
                o_ref[orow, hd * LANE:(hd + 1) * LANE] = (h * g).astype(BF16)


def _rglru(P, conv_w, conv_b, gate_w, gate_b, lam, B, L, Lc):
    H, blk = gate_w.shape[2], gate_w.shape[3]
    d_lru = H * blk
    nh = LRU_HEADS_PER_STEP if H % LRU_HEADS_PER_STEP == 0 else 1
    wid = nh * blk
    assert blk == LANE and Lc % (SUBLANE * SUBLANE) == 0 and L % (SUBLANE * SUBLANE) == 0
    RB = L + Lc
    scr_rows = SUBLANE * (_scan_pitch(Lc // SUBLANE) + _scan_pitch(L // SUBLANE))
    zoff = H // nh
    assert P.shape[2] == blk
    return pl.pallas_call(
        functools.partial(_rglru_kernel, Lc=Lc, L=L, nh=nh),
        grid=(B, H // nh),
        in_specs=[pl.BlockSpec((nh, RB, blk), lambda b, h: (h, b, 0)),
                  pl.BlockSpec((nh, RB, blk), lambda b, h: (zoff + h, b, 0), pipeline_mode=pl.Buffered(1)),
                  pl.BlockSpec((conv_w.shape[0], wid), lambda b, h: (0, h)),
                  pl.BlockSpec((1, wid), lambda b, h: (0, h)),
                  pl.BlockSpec((2, 2, nh, blk, blk), lambda b, h: (0, 0, h, 0, 0)),
                  pl.BlockSpec((4, wid), lambda b, h: (0, h)),
                  pl.BlockSpec((2, wid), lambda b, h: (0, h))],
        out_specs=pl.BlockSpec((RB, wid), lambda b, h: (b, h)),
        out_shape=jax.ShapeDtypeStruct((B * RB, d_lru), BF16),
        scratch_shapes=[pltpu.VMEM((RB + 3 * SUBLANE, wid), F32)]
        + [pltpu.VMEM((2 * nh, scr_rows, LANE), F32)] * 3,
        compiler_params=_cp(2, VMEM_LIMIT_BIG), name="rglru",
    )(P, P, conv_w, conv_b.reshape(1, d_lru), gate_w, gate_b.reshape(4, d_lru), lam)


def _pool_kernel(s_ref, pw_ref, ps_ref, o_ref, pad_scr, *, Lc, L, windows):
    g = pl.program_id(1)
    GW = GRID_W
    NR = L // GW
    PAD = (max(windows) // 2) * GW
    T = ROW_TILE
    pw = pw_ref[...].astype(BF16)
    scale = ps_ref[...]

    def s_rows(rows):
        return jnp.concatenate([s_ref[p, rows, :] for p in range(s_ref.shape[0])], axis=1).astype(F32)

    def finish(mean, r0):
        d = (mean - s_rows(pl.ds(r0, T))).astype(BF16)
        y = jnp.dot(d, pw, preferred_element_type=F32) * scale
        o_ref[pl.ds(r0, T), :] = y.astype(BF16)

    def band_sum(band, v):
        hi = v.astype(BF16)
        lo = (v - hi.astype(F32)).astype(BF16)
        return jnp.dot(band, hi, preferred_element_type=F32) + jnp.dot(band, lo, preferred_element_type=F32)

    def body(w):
        lo_w, hi_w = w // 2, w - w // 2
        ri = lax.broadcasted_iota(I32, (T, T), 0)
        ci = lax.broadcasted_iota(I32, (T, T), 1)
        tcol = lax.broadcasted_iota(I32, (T, 1), 0)

        for t in range(Lc // T):
            cc = lax.broadcasted_iota(I32, (T, Lc), 1)
            rr = lax.broadcasted_iota(I32, (T, Lc), 0) + t * T
            band = ((cc >= rr - lo_w) & (cc < rr + hi_w)).astype(BF16)
            pos = tcol + t * T
            cnt = (jnp.minimum(pos + hi_w, Lc) - jnp.maximum(pos - lo_w, 0)).astype(F32)
            finish(band_sum(band, s_rows(slice(0, Lc))) / cnt, t * T)

        pad_scr[0:PAD, :] = jnp.zeros((PAD, pad_scr.shape[1]), F32)
        pad_scr[PAD + L:PAD + L + PAD, :] = jnp.zeros((PAD, pad_scr.shape[1]), F32)
        for p in range(s_ref.shape[0]):
            pad_scr[PAD:PAD + L, p * LANE:(p + 1) * LANE] = s_ref[p, Lc:Lc + L, :].astype(F32)
        band = ((ri // GW == ci // GW) & (ci % GW >= ri % GW - lo_w) & (ci % GW < ri % GW + hi_w)).astype(BF16)

        def tile(t, carry):
            r0 = pl.multiple_of(t * T, T)
            acc = pad_scr[pl.ds(PAD + r0 - lo_w * GW, T), :]
            for o in range(-lo_w + 1, hi_w):
                acc = acc + pad_scr[pl.ds(PAD + r0 + o * GW, T), :]
            tok = tcol + r0
            gr = tok // GW
            gc = tok % GW
            cnt_r = (jnp.minimum(gr + hi_w, NR) - jnp.maximum(gr - lo_w, 0)).astype(F32)
            cnt_c = (jnp.minimum(gc + hi_w, GW) - jnp.maximum(gc - lo_w, 0)).astype(F32)
            mean = band_sum(band, acc / cnt_r) / cnt_c
            finish(mean, Lc + r0)
            return carry

        lax.fori_loop(0, L // T, tile, 0)

    for gi, w in enumerate(windows):
        pl.when(g == gi)(functools.partial(body, w))


def _pool(P, pool_w, pool_scale, col0, B, L, Lc):
    G, cg = pool_w.shape[0], pool_w.shape[1]
    assert G == len(POOL_WINDOWS) and ROW_TILE % GRID_W == 0 and L % ROW_TILE == 0 and Lc % ROW_TILE == 0
    assert col0 % cg == 0
    RB = L + Lc
    PAD = (max(POOL_WINDOWS) // 2) * GRID_W
    return pl.pallas_call(
        functools.partial(_pool_kernel, Lc=Lc, L=L, windows=POOL_WINDOWS),
        grid=(B, G),
        in_specs=[pl.BlockSpec((cg // LANE, RB, LANE), lambda b, g: (col0 // cg + g, b, 0)),
                  pl.BlockSpec((None, cg, cg), lambda b, g: (g, 0, 0)),
                  pl.BlockSpec((1, cg), lambda b, g: (0, g))],
        out_specs=pl.BlockSpec((RB, cg), lambda b, g: (b, g)),
        out_shape=jax.ShapeDtypeStruct((B * RB, G * cg), BF16),
        scratch_shapes=[pltpu.VMEM((L + 2 * PAD, cg), F32)],
        compiler_params=_cp(2), name="pool",
    )(P, pool_w, pool_scale.reshape(1, G * cg))


def _mlstm_prep_kernel(u_ref, wg_ref, gb_ref, cq_ref, at_ref, *, H):
    T = MLSTM_T
    G = lax.dot_general(u_ref[...], wg_ref[...], (((1,), (1,)), ((), ())), preferred_element_type=F32) + gb_ref[...]
    ls = jax.nn.log_sigmoid(G)
    ri = lax.broadcasted_iota(I32, (T, T), 0)
    ci = lax.broadcasted_iota(I32, (T, T), 1)
    hp = lax.Precision.HIGHEST
    bf = jnp.dot((ci <= ri).astype(F32), ls, precision=hp, preferred_element_type=F32)
    bb = jnp.dot((ci >= ri).astype(F32), ls, precision=hp, preferred_element_type=F32)
    lane = lax.broadcasted_iota(I32, (T, LANE), 1)
    row = lax.broadcasted_iota(I32, (T, LANE), 0)
    fwd_lane = lane < 2 * H
    bsel = jnp.where(fwd_lane, bf, bb)
    a = G - pltpu.roll(bsel, LANE - H, 1)
    pf = a
    pb = a
    k = 1
    while k < T:
        pf = jnp.maximum(pf, jnp.where(row >= k, pltpu.roll(pf, k, 0), -jnp.inf))
        pb = jnp.maximum(pb, jnp.where(row < T - k, pltpu.roll(pb, T - k, 0), -jnp.inf))
        k *= 2
    is_i = (lane < H) | ((lane >= 2 * H) & (lane < 3 * H))
    cq_ref[...] = jnp.where(is_i, jnp.where(fwd_lane, pf, pb), bsel)
    at_ref[...] = a.T


def _mlstm_prep(u, w_gate_nk, gate_b, H):
    R, D = u.shape
    T = MLSTM_T
    gb = jnp.pad(gate_b.reshape(1, 4 * H), ((0, 0), (0, LANE - 4 * H)))
    wg = jnp.pad(w_gate_nk, ((0, LANE - 4 * H), (0, 0))).astype(BF16)
    return pl.pallas_call(
        functools.partial(_mlstm_prep_kernel, H=H),
        grid=(R // T,),
        in_specs=[pl.BlockSpec((T, D), lambda i: (i, 0)), pl.BlockSpec((LANE, D), lambda i: (0, 0)),
                  pl.BlockSpec((1, LANE), lambda i: (0, 0))],
        out_specs=[pl.BlockSpec((T, LANE), lambda i: (i, 0)), pl.BlockSpec((None, LANE, T), lambda i: (i, 0, 0))],
        out_shape=[jax.ShapeDtypeStruct((R, LANE), F32), jax.ShapeDtypeStruct((R // T, LANE, T), F32)],
        compiler_params=_cp(1), name="mlstm_prep",
    )(u, wg, gb)


def _mlstm_kernel(q_ref, kt_ref, v_ref, og_ref, cq_ref, at_ref, ng_ref, y_ref,
                  hs_scr, c_scr, n_scr, m_scr, *, Lc, L, H, lat_only):
    T = MLSTM_T
    hh = pl.program_id(1)
    nc = (Lc + L) // T
    ncc = Lc // T
    scale = q_ref.shape[1] ** -0.5

    def wide(ref, rows):
        return jnp.concatenate([ref[p, rows, :] for p in range(ref.shape[0])], axis=1)

    ri = lax.broadcasted_iota(I32, (T, T), 0)
    ci = lax.broadcasted_iota(I32, (T, T), 1)
    ones_b = jnp.ones((T, LANE), BF16)

    for d in range(2):
        c_scr[...] = jnp.zeros(c_scr.shape, F32)
        n_scr[...] = jnp.zeros(n_scr.shape, F32)
        m_scr[...] = jnp.zeros(m_scr.shape, F32)
        tri = (ci <= ri) if d == 0 else (ci >= ri)
        end = T - 1 if d == 0 else 0

        def chunk(step, carry, d=d, tri=tri, end=end):
            if d == 0:
                c = step
            else:
                c = jnp.where(step < ncc, ncc - 1 - step, nc - 1 - (step - ncc))
            r0 = pl.multiple_of(c * T, T)
            q = q_ref[pl.ds(r0, T), :]
            kt = kt_ref[c]
            v = wide(v_ref, pl.ds(r0, T))
            cq = cq_ref[pl.ds(r0, T), :]
            pm_col = cq[:, 2 * d:2 * d + 1]
            b_col = cq[:, 2 * d + 1:2 * d + 2]
            a_row = at_ref[c, pl.ds(d * 2 * H + hh, 1), :]
            m = m_scr[...]
            mx = jnp.maximum(m, pm_col)
            dec = jnp.exp(jnp.where(tri, a_row - mx, -jnp.inf)) * scale
            s = jnp.dot(q, kt, preferred_element_type=F32)
            sw = (s * dec).astype(BF16)
            e_int = jnp.exp(m - mx) * scale
            num = jnp.dot(sw, v, preferred_element_type=F32) + e_int * jnp.dot(
                q, c_scr[...].astype(BF16), preferred_element_type=F32)
            den = (jnp.dot(sw, ones_b, preferred_element_type=F32)[:, 0:1]
                   + e_int * jnp.dot(q, n_scr[...].astype(BF16), preferred_element_type=F32)[:, 0:1])
            hout = num / jnp.maximum(jnp.abs(den), jnp.exp(-(b_col + mx)))
            if d == 0:
                hs_scr[pl.ds(r0, T), :] = hout
            else:
                hs_scr[pl.ds(r0, T), :] = hs_scr[pl.ds(r0, T), :] + hout
            mxe = jnp.maximum(m, pm_col[end:end + 1, :])
            decay = jnp.exp(m - mxe)
            kw = (kt.astype(F32) * jnp.exp(a_row - mxe)).astype(BF16)
            c_scr[...] = decay * c_scr[...] + jnp.dot(kw, v, preferred_element_type=F32)
            n_scr[...] = decay * n_scr[...] + jnp.dot(kw, ones_b, preferred_element_type=F32)
            m_scr[...] = b_col[end:end + 1, :] + mxe
            return carry

        lax.fori_loop(0, nc, chunk, 0)

    ng = ng_ref[...]
    first = ncc if lat_only else 0

    def finish(c, carry):
        r0 = pl.multiple_of(c * T, T)
        hn = _ln_rows(hs_scr[pl.ds(r0, T), :]) * ng
        y = _sigmoid(wide(og_ref, pl.ds(r0, T)).astype(F32)) * hn
        y_ref[pl.ds(pl.multiple_of(r0 - first * T, T), T), :] = y.astype(BF16)
        return carry

    lax.fori_loop(first, nc, finish, 0)


def _mlstm(QKVO, u, w_gate_nk, gate_b, norm_g, B, L, Lc, H, dqk, dv, lat_only):
    T = MLSTM_T
    RB = L + Lc
    R = B * RB
    nc = RB // T
    assert dqk % LANE == 0 and dv % LANE == 0 and L % T == 0 and Lc % T == 0 and 4 * H <= LANE
    D = H * dv
    out_rows = L if lat_only else RB
    vp = dv // dqk
    assert QKVO.shape[2] == dqk and dv % dqk == 0 and (2 * H) % vp == 0
    voff = (2 * H) // vp
    ooff = voff + H
    cq, at = _mlstm_prep(u, w_gate_nk, gate_b, H)
    cols = cq[:, :4 * H].reshape(R, 4, H).transpose(2, 0, 1)
    kt = QKVO[H:2 * H].reshape(H, R // T, T, dqk).transpose(1, 0, 3, 2).reshape(R // T, H * dqk, T)
    return pl.pallas_call(
        functools.partial(_mlstm_kernel, Lc=Lc, L=L, H=H, lat_only=lat_only),
        grid=(B, H),
        in_specs=[pl.BlockSpec((None, RB, dqk), lambda b, h: (h, b, 0)),
                  pl.BlockSpec((nc, dqk, T), lambda b, h: (b, h, 0)),
                  pl.BlockSpec((vp, RB, dqk), lambda b, h: (voff + h, b, 0)),
                  pl.BlockSpec((vp, RB, dqk), lambda b, h: (ooff + h, b, 0)),
                  pl.BlockSpec((None, RB, 4), lambda b, h: (h, b, 0)),
                  pl.BlockSpec((nc, LANE, T), lambda b, h: (b, 0, 0)),
                  pl.BlockSpec((1, dv), lambda b, h: (0, h))],
        out_specs=pl.BlockSpec((out_rows, dv), lambda b, h: (b, h)),
        out_shape=jax.ShapeDtypeStruct((B * out_rows, D), BF16),
        scratch_shapes=[pltpu.VMEM((RB, dv), F32), pltpu.VMEM((dqk, dv), F32),
                        pltpu.VMEM((dqk, LANE), F32), pltpu.VMEM((1, 1), F32)],
        compiler_params=_cp(2, VMEM_LIMIT_BIG), name="mlstm",
    )(QKVO, kt, QKVO, QKVO, cols, at, norm_g.reshape(1, D))


def _ln_mid_kernel(h_ref, y_ref, mod_ref, g_ref, b_ref, wr_ref, h_out, u_out, aff_out, ub_scr, inv, *, alpha, n_exp):
    _stage_rows(inv, [mod_ref[2:3, :], g_ref[...], b_ref[...], 1.0 + mod_ref[4:5, :], mod_ref[3:4, :]])

    def blk(r0):
        rows16 = pl.ds(r0, LN_ROWS)
        y = y_ref[rows16, :].astype(F32)
        us = []
        for k, rows in enumerate(_halves(r0)):
            v = alpha * h_ref[rows, :] + inv[0] * y[k * SUBLANE:(k + 1) * SUBLANE]
            h1 = _ln_rows(v) * inv[1] + inv[2]
            h_out[rows, :] = h1
            us.append(h1 * inv[3] + inv[4])
        u2 = jnp.concatenate(us, axis=0)
        ub_scr[rows16, :] = u2.astype(BF16)
        u_out[rows16, :] = u2

    _row_blocks(ROW_TILE, blk)
    logits = jnp.dot(ub_scr[...], wr_ref[...], preferred_element_type=F32)
    lane = lax.broadcasted_iota(I32, logits.shape, 1)
    logits = jnp.where(lane < n_exp, logits, -jnp.inf)
    e = jnp.exp(logits - jnp.max(logits, axis=-1, keepdims=True))
    aff_out[...] = e / jnp.sum(e, axis=-1, keepdims=True)


def _ln_mid(h, y, mods, layer, ln_g, ln_b, w_router, alpha, B, L, Lc, lat_only):
    D = h.shape[-1]
    E = w_router.shape[1]
    tpb = (L + Lc) // ROW_TILE
    lt = L // ROW_TILE
    wr = jnp.pad(w_router, ((0, 0), (0, LANE - E))).astype(BF16)
    if lat_only:
        n_tiles = B * lt
        h_map = lambda i: (i + 1 + i // lt, 0)
        mod_map = lambda i: (layer, i // lt, 0, 0)
    else:
        n_tiles = B * tpb
        h_map = lambda i: (i, 0)
        mod_map = lambda i: (layer, jnp.where(i % tpb == 0, B, i // tpb), 0, 0)
    rows = n_tiles * ROW_TILE
    return pl.pallas_call(
        functools.partial(_ln_mid_kernel, alpha=alpha, n_exp=E),
        grid=(n_tiles,),
        in_specs=[pl.BlockSpec((ROW_TILE, D), h_map),
                  pl.BlockSpec((ROW_TILE, D), lambda i: (i, 0)),
                  pl.BlockSpec((None, None, 6, D), mod_map),
                  pl.BlockSpec((1, D), lambda i: (0, 0)),
                  pl.BlockSpec((1, D), lambda i: (0, 0)),
                  pl.BlockSpec((D, LANE), lambda i: (0, 0))],
        out_specs=[pl.BlockSpec((ROW_TILE, D), lambda i: (i, 0)),
                   pl.BlockSpec((ROW_TILE, D), lambda i: (i, 0)),
                   pl.BlockSpec((ROW_TILE, LANE), lambda i: (i, 0))],
        out_shape=[jax.ShapeDtypeStruct((rows, D), F32), jax.ShapeDtypeStruct((rows, D), F32),
                   jax.ShapeDtypeStruct((rows, LANE), F32)],
        scratch_shapes=[pltpu.VMEM((ROW_TILE, D), BF16), pltpu.VMEM((5, SUBLANE, D), F32)],
        compiler_params=_cp(1), name="ln_mid",
    )(h, y, mods, ln_g.reshape(1, D), ln_b.reshape(1, D), wr)


def _moe_gu_kernel(rows_ref, u_hbm, wg_ref, wu_ref, o_ref, xg, xb, sem, *, ns, nf_static):
    e = pl.program_id(0)
    f = pl.program_id(1)
    ne = pl.num_programs(0)
    nf = pl.num_programs(1)
    per = ns // nf_static

    def row_copy(table_row, dst_row):
        r = rows_ref[table_row]
        return pltpu.make_async_copy(u_hbm.at[pl.ds(r, 1), :], xg.at[pl.ds(dst_row, 1), :], sem.at[0])

    def wait_all():
        pltpu.make_async_copy(u_hbm.at[pl.ds(0, ns), :], xg, sem.at[0]).wait()

    @pl.when((e == 0) & (f == 0))
    def _():
        def body(s, carry):
            row_copy(s, s).start()
            return carry
        lax.fori_loop(0, ns, body, 0, unroll=8)

    @pl.when(f == 0)
    def _():
        wait_all()
        xb[...] = xg[...].astype(BF16)

    first = (e + 1) * ns + f * per
    for s in range(per):
        row_copy(first + s, f * per + s).start()

    x = xb[...]
    hg = jnp.dot(x, wg_ref[...].astype(BF16), preferred_element_type=F32)
    hu = jnp.dot(x, wu_ref[...].astype(BF16), preferred_element_type=F32)
    o_ref[...] = (hg * _sigmoid(hg) * hu).astype(BF16)

    @pl.when((e == ne - 1) & (f == nf - 1))
    def _():
        wait_all()


def _moe_gu(rows, u2p, w_gu, layer):
    _, E, D, F2 = w_gu.shape
    F = F2 // 2
    ns = rows.shape[1]
    fc = min(MOE_FC, F)
    nf = F // fc
    assert ns % 8 == 0 and ns % nf == 0
    rows = jnp.concatenate([rows, rows[:1]], axis=0)
    grid_spec = pltpu.PrefetchScalarGridSpec(
        num_scalar_prefetch=1,
        grid=(E, nf),
        in_specs=[pl.BlockSpec(memory_space=pl.ANY),
                  pl.BlockSpec((None, None, D, fc), lambda e, f, rows: (layer, e, 0, f)),
                  pl.BlockSpec((None, None, D, fc), lambda e, f, rows: (layer, e, 0, nf + f))],
        out_specs=pl.BlockSpec((None, ns, fc), lambda e, f, rows: (e, 0, f)),
        scratch_shapes=[pltpu.VMEM((ns, D), F32), pltpu.VMEM((ns, D), BF16),
                        pltpu.SemaphoreType.DMA((1,))],
    )
    return pl.pallas_call(
        functools.partial(_moe_gu_kernel, ns=ns, nf_static=nf),
        grid_spec=grid_spec,
        out_shape=jax.ShapeDtypeStruct((E, ns, F), BF16),
        compiler_params=_cp(2), name="moe_gu",
    )(rows.reshape(-1), u2p, w_gu, w_gu)


def _moe_down_kernel(h_ref, w_ref, o_ref):
    o_ref[...] = jnp.dot(h_ref[...], w_ref[...].astype(BF16), preferred_element_type=F32).astype(BF16)


def _moe_down(hmid, w_down, layer):
    _, E, F, D = w_down.shape
    ns = hmid.shape[1]
    tn = min(MOE_TN, D)
    return pl.pallas_call(
        _moe_down_kernel,
        grid=(E, D // tn),
        in_specs=[pl.BlockSpec((None, ns, F), lambda e, j: (e, 0, 0)),
                  pl.BlockSpec((None, None, F, tn), lambda e, j: (layer, e, 0, j))],
        out_specs=pl.BlockSpec((None, ns, tn), lambda e, j: (e, 0, j)),
        out_shape=jax.ShapeDtypeStruct((E, ns, D), BF16),
        compiler_params=_cp(2), name="moe_down",
    )(hmid, w_down)


def _combine_kernel(a_ref, np_ref, y_hbm, pos_ref, aff_ref, av_ref, h_ref, mod_ref, modn_ref, g_ref, b_ref,
                    h_out, u_out, ybuf, acc, inv, sem, *, n_exp, ns, alpha, final):
    i = pl.program_id(0)
    n_tiles = pl.num_programs(0)
    slot = i % 2
    W = COMB_W
    pos = pos_ref[...]
    a_v = av_ref[...]
    gate = aff_ref[...].astype(BF16)
    sub = lax.broadcasted_iota(I32, (LANE, n_exp * W), 0)
    lan = lax.broadcasted_iota(I32, (LANE, n_exp * W), 1)
    expand = (lan // W == sub).astype(BF16)
    kk = (lax.broadcasted_iota(I32, (1, n_exp * W), 1) % W).astype(F32)

    def strip(tile, p, e, sl):
        w0 = pl.multiple_of(jnp.minimum(a_ref[tile * n_exp + e] + W * p, ns - W), COMB_ALIGN)
        return pltpu.make_async_copy(y_hbm.at[e, pl.ds(w0, W), :], ybuf.at[sl, pl.ds(e * W, W), :], sem.at[sl])

    def fetch(tile, p, sl):
        for e in range(n_exp):
            strip(tile, p, e, sl).start()

    def land(tile, p, sl):
        for e in range(n_exp):
            strip(tile, p, e, sl).wait()

    gate_x = jnp.dot(gate, expand, preferred_element_type=F32)

    def selector(p):
        start = a_v + W * p
        valid = (pos >= start) & (pos < start + W)
        rel = jnp.where(valid, pos - jnp.minimum(start, ns - W), -1).astype(F32).astype(BF16)
        relx = jnp.dot(rel, expand, preferred_element_type=F32)
        return jnp.where(relx == kk, gate_x, 0.0).astype(BF16)

    @pl.when(i == 0)
    def _():
        fetch(0, 0, 0)

    @pl.when(i + 1 < n_tiles)
    def _():
        fetch(i + 1, 0, 1 - slot)

    sel0 = selector(0)
    land(i, 0, slot)
    acc[...] = jnp.dot(sel0, ybuf[slot], preferred_element_type=F32)

    def extra_pass(p, carry):
        fetch(i, p, slot)
        selp = selector(p)
        land(i, p, slot)
        acc[...] += jnp.dot(selp, ybuf[slot], preferred_element_type=F32)
        return carry

    lax.fori_loop(1, np_ref[i], extra_pass, 0)

    vecs = [mod_ref[5:6, :], g_ref[...], b_ref[...]]
    if not final:
        vecs += [1.0 + modn_ref[1:2, :], modn_ref[0:1, :]]
    _stage_rows(inv, vecs)

    def blk(r0):
        us = []
        for rows in _halves(r0):
            v = alpha * h_ref[rows, :] + inv[0] * acc[rows, :]
            h2 = _ln_rows(v) * inv[1] + inv[2]
            h_out[rows, :] = h2
            if not final:
                us.append(h2 * inv[3] + inv[4])
        if not final:
            u_out[pl.ds(r0, LN_ROWS), :] = jnp.concatenate(us, axis=0).astype(BF16)

    _row_blocks(ROW_TILE, blk)


def _combine(Y, pos, aff, a_tab, npass, h1, mods, layer, ln_g, ln_b, alpha, B, L, Lc, lat_only, final):
    E, ns, D = Y.shape
    tpb = (L + Lc) // ROW_TILE
    lt = L // ROW_TILE
    n_tiles = h1.shape[0] // ROW_TILE
    nxt = layer if final else layer + 1
    if lat_only:
        sel = lambda i: i // lt
    else:
        sel = lambda i: jnp.where(i % tpb == 0, B, i // tpb)
    a_pad = a_tab.reshape(n_tiles, 1, LANE)
    out_shape = [jax.ShapeDtypeStruct((n_tiles * ROW_TILE, D), F32)]
    out_specs = [pl.BlockSpec((ROW_TILE, D), lambda i, a, n: (i, 0))]
    if not final:
        out_shape.append(jax.ShapeDtypeStruct((n_tiles * ROW_TILE, D), BF16))
        out_specs.append(pl.BlockSpec((ROW_TILE, D), lambda i, a, n: (i, 0)))
    grid_spec = pltpu.PrefetchScalarGridSpec(
        num_scalar_prefetch=2,
        grid=(n_tiles,),
        in_specs=[pl.BlockSpec(memory_space=pl.ANY),
                  pl.BlockSpec((ROW_TILE, LANE), lambda i, a, n: (i, 0)),
                  pl.BlockSpec((ROW_TILE, LANE), lambda i, a, n: (i, 0)),
                  pl.BlockSpec((None, 1, LANE), lambda i, a, n: (i, 0, 0)),
                  pl.BlockSpec((ROW_TILE, D), lambda i, a, n: (i, 0)),
                  pl.BlockSpec((None, None, 6, D), lambda i, a, n: (layer, sel(i), 0, 0)),
                  pl.BlockSpec((None, None, 6, D), lambda i, a, n: (nxt, sel(i), 0, 0)),
                  pl.BlockSpec((1, D), lambda i, a, n: (0, 0)),
                  pl.BlockSpec((1, D), lambda i, a, n: (0, 0))],
        out_specs=out_specs,
        scratch_shapes=[pltpu.VMEM((2, E * COMB_W, D), BF16), pltpu.VMEM((ROW_TILE, D), F32),
                        pltpu.VMEM((3 if final else 5, SUBLANE, D), F32), pltpu.SemaphoreType.DMA((2,))],
    )

    def kern(*refs):
        if final:
            _combine_kernel(*refs[:12], None, *refs[12:], n_exp=E, ns=ns, alpha=alpha, final=True)
        else:
            _combine_kernel(*refs, n_exp=E, ns=ns, alpha=alpha, final=False)

    return pl.pallas_call(
        kern, grid_spec=grid_spec, out_shape=out_shape,
        compiler_params=_cp(1), name="moe_combine",
    )(a_tab[:, :E].reshape(-1), npass, Y, pos, aff, a_pad, h1, mods, mods, ln_g.reshape(1, D), ln_b.reshape(1, D))


def _route(aff, E, B, L, Lc, lat_only):
    RB = L if lat_only else L + Lc
    n_rows = B * RB
    a3 = aff[:, :E].reshape(B, RB, E)

    def pick(a, n, off, slot0):
        cap = CAPACITY_FACTOR * n // E
        at = jnp.swapaxes(a, 1, 2)
        top, idx = lax.top_k(at, cap)
        idx = jnp.sort(idx, axis=-1)
        rows = idx + off + (jnp.arange(B, dtype=I32) * RB)[:, None, None]
        thr = top[..., cap - 1:cap]
        gt = at > thr
        eq = at == thr
        need = cap - jnp.sum(gt, axis=-1, keepdims=True)
        sel = gt | (eq & (jnp.cumsum(eq, axis=-1) <= need))
        slot = jnp.cumsum(sel, axis=-1) - 1 + slot0 + (jnp.arange(B, dtype=I32) * cap)[:, None, None]
        pos = jnp.swapaxes(jnp.where(sel, slot, -1), 1, 2).astype(I32)
        return jnp.swapaxes(rows, 0, 1).reshape(E, B * cap), pos

    if lat_only:
        rows, pos = pick(a3, L, 0, 0)
    else:
        r_l, p_l = pick(a3[:, Lc:], L, Lc, 0)
        r_c, p_c = pick(a3[:, :Lc], Lc, 0, r_l.shape[1])
        rows = jnp.concatenate([r_l, r_c], axis=1)
        pos = jnp.concatenate([p_c, p_l], axis=1)
    ns = rows.shape[1]
    assert ns % COMB_ALIGN == 0 and ns >= COMB_W
    pos = jnp.pad(pos.reshape(n_rows, E), ((0, 0), (0, LANE - E)), constant_values=-1)
    pt = pos.reshape(n_rows // ROW_TILE, ROW_TILE, LANE)
    has = pt >= 0
    cnt = jnp.sum(has, axis=1)
    lo = jnp.min(jnp.where(has, pt, ns), axis=1)
    lo = jnp.where(cnt > 0, lo, 0)
    a_tab = (lo // COMB_ALIGN) * COMB_ALIGN
    span = lo + cnt - a_tab
    npass = jnp.maximum(jnp.max((span + COMB_W - 1) // COMB_W, axis=1), 1).astype(I32)
    return rows.astype(I32), pos, a_tab.astype(I32), npass


def kernel(x, c, ctx, c_ctx, ada_w, ada_b, ln_g, ln_b, even_w_in, even_conv_w, even_conv_b, lru_gate_w,
           lru_gate_b, lru_lambda, pool_w, pool_scale, even_w_out, odd_w_in, odd_gate_b, odd_norm_g,
           odd_w_out, router_w, expert_w_gu, expert_w_down):
    B, L, D = x.shape
    Lc = ctx.shape[1]
    depth = ada_w.shape[0]
    E = router_w.shape[2]
    assert Lc == ROW_TILE and L % ROW_TILE == 0 and B + 1 <= 8 and depth >= 2
    alpha = (2 * depth) ** 0.25

    cond8 = jnp.zeros((8, D), F32).at[:B].set(c).at[B].set(c_ctx)
    mods = _adaln(cond8, ada_w, ada_b).reshape(depth, 8, 6, D)

    h, u = _ln_in(x.reshape(B * L, D), ctx.reshape(B * Lc, D), mods, B, L, Lc)

    for layer in range(depth):
        j = layer // 2
        last = layer == depth - 1
        if layer % 2 == 0:
            d_lru = lru_gate_w.shape[3] * lru_gate_w.shape[4]
            P = _matmul([u], even_w_in, j, even_w_in.shape[2], BF16, piece=LANE, name="even_w_in")
            mix_lru = _rglru(P, even_conv_w[j], even_conv_b[j], lru_gate_w[j], lru_gate_b[j], lru_lambda[j], B, L, Lc)
            mix_pool = _pool(P, pool_w[j], pool_scale[j], 2 * d_lru, B, L, Lc)
            if last:
                keep = lambda t: t.reshape(B, L + Lc, -1)[:, Lc:].reshape(B * L, -1)
                mix_lru, mix_pool = keep(mix_lru), keep(mix_pool)
            y = _matmul([mix_lru, mix_pool], even_w_out, j, D, BF16, name="even_w_out")
        else:
            H = odd_gate_b.shape[1] // 4
            dv = D // H
            dqk = dv // 2
            n_main = 2 * H * dqk + 2 * D
            w_in_t = jnp.swapaxes(odd_w_in, 1, 2)
            QKVO = _matmul([u], w_in_t, j, n_main, BF16, w_is_nk=True, piece=dqk, name="odd_w_in")
            hm = _mlstm(QKVO, u, w_in_t[j, n_main:, :], odd_gate_b[j], odd_norm_g[j], B, L, Lc, H, dqk, dv, last)
            y = _matmul([hm], odd_w_out, j, D, BF16, name="odd_w_out")

        h1, u2p, aff = _ln_mid(h, y, mods, layer, ln_g[layer, 0], ln_b[layer, 0], router_w[layer], alpha,
                               B, L, Lc, last)
        rows, pos, a_tab, npass = _route(aff, E, B, L, Lc, last)
        hmid = _moe_gu(rows, u2p, expert_w_gu, layer)
        Y = _moe_down(hmid, expert_w_down, layer)
        outs = _combine(Y, pos, aff, a_tab, npass, h1, mods, layer, ln_g[layer, 1], ln_b[layer, 1], alpha,
                        B, L, Lc, last, last)
        if last:
            return outs[0].reshape(B, L, D)
        h, u = outs
```

```python
import functools

import jax
import jax.numpy as jnp
from jax import lax
from jax.experimental import pallas as pl
from jax.experimental.pallas import tpu as pltpu

F32 = jnp.float32
BF16 = jnp.bfloat16
I32 = jnp.int32

LN_EPS = 1e-5
TINY = 1e-30
GRID_W = 64
LRU_C = 8.0
POOL_WINDOWS = (2, 4, 8, 16)
CAPACITY_FACTOR = 2

LANE = 128
SUBLANE = 8
LN_ROWS = 64
ROW_TILE = 256
MM_TM = 512
MM_TN = 1024
MM_KC = 512
LRU_HEADS_PER_STEP = 2
SCAN_UNROLL = 8
MLSTM_T = 256
MOE_FC = 256
MOE_TN = 2048
COMB_W_SHIFT = 6
COMB_W = 1 << COMB_W_SHIFT
COMB_ALIGN_SHIFT = 4
COMB_ALIGN = 1 << COMB_ALIGN_SHIFT
VMEM_LIMIT = 56 * 1024 * 1024
VMEM_LIMIT_BIG = 60 * 1024 * 1024


def _cp(n_axes, vmem=VMEM_LIMIT):
    return pltpu.CompilerParams(dimension_semantics=("arbitrary",) * n_axes, vmem_limit_bytes=vmem)


def _ln_rows(v):
    mu = jnp.mean(v, axis=-1, keepdims=True)
    d = v - mu
    var = jnp.mean(d * d, axis=-1, keepdims=True)
    return d * lax.rsqrt(var + LN_EPS)


def _sigmoid(x):
    return 0.5 + 0.5 * jnp.tanh(0.5 * x)


def _stage_rows(scr, vecs):
    for k, v in enumerate(vecs):
        scr[k] = jnp.broadcast_to(v, scr.shape[1:])


def _row_blocks(n_rows, body):
    def step(i, carry):
        body(pl.multiple_of(i * LN_ROWS, LN_ROWS))
        return carry
    lax.fori_loop(0, n_rows // LN_ROWS, step, 0)


def _halves(r0):
    return [pl.ds(r0 + k * SUBLANE, SUBLANE) for k in range(LN_ROWS // SUBLANE)]


def _adaln_kernel(c_ref, w_ref, b_ref, o_ref):
    c = c_ref[...]
    s = (c * _sigmoid(c)).astype(BF16)
    o_ref[...] = jnp.dot(s, w_ref[...].astype(BF16), preferred_element_type=F32) + b_ref[...]


def _adaln(cond8, ada_w, ada_b):
    depth, D, N = ada_w.shape
    tn = 512
    return pl.pallas_call(
        _adaln_kernel,
        grid=(depth, N // tn),
        in_specs=[pl.BlockSpec((8, D), lambda l, j: (0, 0)),
                  pl.BlockSpec((None, D, tn), lambda l, j: (l, 0, j)),
                  pl.BlockSpec((None, 1, tn), lambda l, j: (l, 0, j))],
        out_specs=pl.BlockSpec((None, 8, tn), lambda l, j: (l, 0, j)),
        out_shape=jax.ShapeDtypeStruct((depth, 8, N), F32),
        compiler_params=_cp(2), name="adaln",
    )(cond8, ada_w, ada_b.reshape(depth, 1, N))


def _ln_in_kernel(xl_ref, xc_ref, mod_ref, u_ref, inv, *, tpb):
    is_ctx = (pl.program_id(0) % tpb) == 0
    _stage_rows(inv, [1.0 + mod_ref[1:2, :], mod_ref[0:1, :]])

    def go(src):
        def blk(r0):
            us = [_ln_rows(src[rows, :]) * inv[0] + inv[1] for rows in _halves(r0)]
            u_ref[pl.ds(r0, LN_ROWS), :] = jnp.concatenate(us, axis=0).astype(BF16)
        _row_blocks(ROW_TILE, blk)

    pl.when(is_ctx)(lambda: go(xc_ref))
    pl.when(jnp.logical_not(is_ctx))(lambda: go(xl_ref))


def _ln_in(xl, xc, mods, B, L, Lc):
    D = xl.shape[-1]
    tpb = (L + Lc) // ROW_TILE
    lt = L // ROW_TILE
    R = B * (L + Lc)
    return pl.pallas_call(
        functools.partial(_ln_in_kernel, tpb=tpb),
        grid=(B * tpb,),
        in_specs=[pl.BlockSpec((ROW_TILE, D), lambda i: ((i // tpb) * lt + jnp.maximum(i % tpb - 1, 0), 0)),
                  pl.BlockSpec((ROW_TILE, D), lambda i: (i // tpb, 0)),
                  pl.BlockSpec((None, None, 6, D), lambda i: (0, jnp.where(i % tpb == 0, B, i // tpb), 0, 0))],
        out_specs=pl.BlockSpec((ROW_TILE, D), lambda i: (i, 0)),
        out_shape=jax.ShapeDtypeStruct((R, D), BF16),
        scratch_shapes=[pltpu.VMEM((2, SUBLANE, D), F32)],
        compiler_params=_cp(1), name="ln_in",
    )(xl, xc, mods)


def _mm_kernel(*refs, ksplits, w_is_nk, piece):
    nx = len(ksplits)
    x_refs, w_ref, o_ref, wb_ref = refs[:nx], refs[nx], refs[nx + 1], refs[nx + 2]

    @pl.when(pl.program_id(1) == 0)
    def _():
        if w_is_nk:
            for k0 in range(0, wb_ref.shape[0], MM_KC):
                wb_ref[k0:k0 + MM_KC, :] = w_ref[:, k0:k0 + MM_KC].T.astype(BF16)
        else:
            wb_ref[...] = w_ref[...].astype(BF16)

    acc = None
    for x_ref, (k0, k1) in zip(x_refs, ksplits):
        p = jnp.dot(x_ref[...], wb_ref[k0:k1, :], preferred_element_type=F32)
        acc = p if acc is None else acc + p
    if piece is None:
        o_ref[...] = acc.astype(o_ref.dtype)
    else:
        for c in range(o_ref.shape[0]):
            o_ref[c] = acc[:, c * piece:(c + 1) * piece].astype(o_ref.dtype)


def _matmul(xs, w, widx, n_cols, out_dtype, w_is_nk=False, piece=None, tm=MM_TM, name="mm"):
    R = xs[0].shape[0]
    K = w.shape[2] if w_is_nk else w.shape[1]
    tn = next(t for t in (MM_TN, MM_TN // 2, MM_TN // 4, LANE) if n_cols % t == 0)
    ksplits, k0 = [], 0
    for x in xs:
        ksplits.append((k0, k0 + x.shape[1]))
        k0 += x.shape[1]
    assert k0 == K and R % tm == 0 and K % MM_KC == 0
    if w_is_nk:
        w_spec = pl.BlockSpec((None, tn, K), lambda j, i: (widx, j, 0))
    else:
        w_spec = pl.BlockSpec((None, K, tn), lambda j, i: (widx, 0, j))
    if piece is None:
        out_spec = pl.BlockSpec((tm, tn), lambda j, i: (i, j))
        out_shape = jax.ShapeDtypeStruct((R, n_cols), out_dtype)
    else:
        assert tn % piece == 0
        out_spec = pl.BlockSpec((tn // piece, tm, piece), lambda j, i: (j, i, 0))
        out_shape = jax.ShapeDtypeStruct((n_cols // piece, R, piece), out_dtype)
    return pl.pallas_call(
        functools.partial(_mm_kernel, ksplits=tuple(ksplits), w_is_nk=w_is_nk, piece=piece),
        grid=(n_cols // tn, R // tm),
        in_specs=[pl.BlockSpec((tm, x.shape[1]), lambda j, i: (i, 0)) for x in xs] + [w_spec],
        out_specs=out_spec,
        out_shape=out_shape,
        scratch_shapes=[pltpu.VMEM((K, tn), BF16)],
        compiler_params=_cp(2, VMEM_LIMIT_BIG), name=name,
    )(*xs, w)


def _scan_pitch(seg):
    return seg + SUBLANE


def _rglru_kernel(x_ref, z_ref, cw_ref, cb_ref, gw_ref, gb_ref, lam_ref, o_ref,
                  xs_scr, a_scr, b_scr, o_scr, *, Lc, L, nh):
    cw = cw_ref[...]
    cb = cb_ref[...]
    gb_half = 0.5 * gb_ref[...]
    gw_half = [[[(0.5 * gw_ref[d, g, hd]).astype(BF16) for hd in range(nh)] for g in range(2)] for d in range(2)]
    half_c_ls = (0.5 * LRU_C) * jax.nn.log_sigmoid(lam_ref[...])
    width = nh * LANE
    pad = SUBLANE
    off_c, off_l = pad, 2 * pad + Lc
    sequences = ((0, Lc, 0, off_c), (Lc, L, SUBLANE * _scan_pitch(Lc // SUBLANE), off_l))

    zpad = jnp.zeros((pad, width), F32)
    xs_scr[0:pad, :] = zpad
    xs_scr[off_c + Lc:off_c + Lc + pad, :] = zpad
    xs_scr[off_l + L:off_l + L + pad, :] = zpad
    for hd in range(nh):
        xs_scr[off_c:off_c + Lc, hd * LANE:(hd + 1) * LANE] = x_ref[hd, 0:Lc, :].astype(F32)
        xs_scr[off_l:off_l + L, hd * LANE:(hd + 1) * LANE] = x_ref[hd, Lc:Lc + L, :].astype(F32)

    for r0, n, base, off in sequences:
        seg = n // SUBLANE
        pitch = _scan_pitch(seg)
        for j in range(SUBLANE):
            s0 = off + j * seg
            xc = (cw[0:1] * xs_scr[s0 - 1:s0 - 1 + seg, :] + cw[1:2] * xs_scr[s0:s0 + seg, :]
                  + cw[2:3] * xs_scr[s0 + 1:s0 + 1 + seg, :] + cw[3:4] * xs_scr[s0 + 2:s0 + 2 + seg, :] + cb)
            xb = xc.astype(BF16)
            for hd in range(nh):
                cols = slice(hd * LANE, (hd + 1) * LANE)
                xh = xc[:, cols]
                xbh = xb[:, cols]
                for d in range(2):
                    tr = jnp.tanh(jnp.dot(xbh, gw_half[d][0][hd], preferred_element_type=F32) + gb_half[2 * d:2 * d + 1, cols])
                    ti = jnp.tanh(jnp.dot(xbh, gw_half[d][1][hd], preferred_element_type=F32) + gb_half[2 * d + 1:2 * d + 2, cols])
                    log_a = half_c_ls[d:d + 1, cols] * (1.0 + tr)
                    th = jnp.tanh(log_a)
                    p = -2.0 * th
                    mult = p * lax.rsqrt(jnp.maximum(p * (1.0 - th), TINY))
                    bx = mult * ((0.5 + 0.5 * ti) * xh)
                    rows = slice(base + j * pitch, base + j * pitch + seg)
                    a_scr[d * nh + hd, rows, :] = jnp.exp(log_a)
                    b_scr[d * nh + hd, rows, :] = bx

    rid = lax.broadcasted_iota(I32, (SUBLANE, LANE), 0)

    def carries(H, A, init, reverse):
        c = init
        cv = jnp.broadcast_to(init, (SUBLANE, LANE))
        order = range(SUBLANE - 1, -1, -1) if reverse else range(SUBLANE)
        for j in order:
            cv = jnp.where(rid == j, c, cv)
            c = A[j:j + 1] * c + H[j:j + 1]
        return cv, c

    zero = jnp.zeros((SUBLANE, LANE), F32)
    one = jnp.ones((SUBLANE, LANE), F32)
    state = [jnp.zeros((1, LANE), F32)] * (2 * nh)
    for r0, n, base, off in sequences:
        seg = n // SUBLANE
        pitch = _scan_pitch(seg)

        def rows_at(k, t, base=base, seg=seg, pitch=pitch):
            return pl.ds(base + (t if k < nh else seg - 1 - t), SUBLANE, stride=pitch)

        def pair(k, i):
            a0 = a_scr[k, rows_at(k, 2 * i), :]
            b0 = b_scr[k, rows_at(k, 2 * i), :]
            a1 = a_scr[k, rows_at(k, 2 * i + 1), :]
            return a0, b0, a1 * a0, a1 * b0 + b_scr[k, rows_at(k, 2 * i + 1), :]

        def ends(i, c):
            out = []
            for k in range(2 * nh):
                _, _, a2, b2 = pair(k, i)
                out += [a2 * c[2 * k] + b2, a2 * c[2 * k + 1]]
            return tuple(out)

        fin = lax.fori_loop(0, seg // 2, ends, (zero, one) * (2 * nh), unroll=SCAN_UNROLL)
        cvs = []
        for k in range(2 * nh):
            cv, state[k] = carries(fin[2 * k], fin[2 * k + 1], state[k], k >= nh)
            cvs.append(cv)

        def full(i, c):
            out = []
            for k in range(2 * nh):
                a0, b0, a2, b2 = pair(k, i)
                o_scr[k, rows_at(k, 2 * i), :] = a0 * c[k] + b0
                h = a2 * c[k] + b2
                o_scr[k, rows_at(k, 2 * i + 1), :] = h
                out.append(h)
            return tuple(out)

        lax.fori_loop(0, seg // 2, full, tuple(cvs), unroll=SCAN_UNROLL)

        for j in range(SUBLANE):
            rows = slice(base + j * pitch, base + j * pitch + seg)
            orow = slice(r0 + j * seg, r0 + (j + 1) * seg)
            for hd in range(nh):
                h = o_scr[hd, rows, :] + o_scr[nh + hd, rows, :]
                g = jax.nn.gelu(z_ref[hd, orow, :].astype(F32))
                o_ref[orow, hd * LANE:(hd + 1) * LANE] = (h * g).astype(BF16)


def _rglru(P, conv_w, conv_b, gate_w, gate_b, lam, B, L, Lc):
    H, blk = gate_w.shape[2], gate_w.shape[3]
    d_lru = H * blk
    nh = LRU_HEADS_PER_STEP if H % LRU_HEADS_PER_STEP == 0 else 1
    wid = nh * blk
    assert blk == LANE and Lc % (SUBLANE * SUBLANE) == 0 and L % (SUBLANE * SUBLANE) == 0
    RB = L + Lc
    scr_rows = SUBLANE * (_scan_pitch(Lc // SUBLANE) + _scan_pitch(L // SUBLANE))
    zoff = H // nh
    assert P.shape[2] == blk
    return pl.pallas_call(
        functools.partial(_rglru_kernel, Lc=Lc, L=L, nh=nh),
        grid=(B, H // nh),
        in_specs=[pl.BlockSpec((nh, RB, blk), lambda b, h: (h, b, 0)),
                  pl.BlockSpec((nh, RB, blk), lambda b, h: (zoff + h, b, 0), pipeline_mode=pl.Buffered(1)),
                  pl.BlockSpec((conv_w.shape[0], wid), lambda b, h: (0, h)),
                  pl.BlockSpec((1, wid), lambda b, h: (0, h)),
                  pl.BlockSpec((2, 2, nh, blk, blk), lambda b, h: (0, 0, h, 0, 0)),
                  pl.BlockSpec((4, wid), lambda b, h: (0, h)),
                  pl.BlockSpec((2, wid), lambda b, h: (0, h))],
        out_specs=pl.BlockSpec((RB, wid), lambda b, h: (b, h)),
        out_shape=jax.ShapeDtypeStruct((B * RB, d_lru), BF16),
        scratch_shapes=[pltpu.VMEM((RB + 3 * SUBLANE, wid), F32)]
        + [pltpu.VMEM((2 * nh, scr_rows, LANE), F32)] * 3,
        compiler_params=_cp(2, VMEM_LIMIT_BIG), name="rglru",
    )(P, P, conv_w, conv_b.reshape(1, d_lru), gate_w, gate_b.reshape(4, d_lru), lam)


def _pool_kernel(s_ref, pw_ref, ps_ref, o_ref, pad_scr, *, Lc, L, windows):
    g = pl.program_id(1)
    GW = GRID_W
    NR = L // GW
    PAD = (max(windows) // 2) * GW
    T = ROW_TILE
    pw = pw_ref[...].astype(BF16)
    scale = ps_ref[...]

    def s_rows(rows):
        return jnp.concatenate([s_ref[p, rows, :] for p in range(s_ref.shape[0])], axis=1).astype(F32)

    def finish(mean, r0):
        d = (mean - s_rows(pl.ds(r0, T))).astype(BF16)
        y = jnp.dot(d, pw, preferred_element_type=F32) * scale
        o_ref[pl.ds(r0, T), :] = y.astype(BF16)

    def band_sum(band, v):
        hi = v.astype(BF16)
        lo = (v - hi.astype(F32)).astype(BF16)
        return jnp.dot(band, hi, preferred_element_type=F32) + jnp.dot(band, lo, preferred_element_type=F32)

    def body(w):
        lo_w, hi_w = w // 2, w - w // 2
        ri = lax.broadcasted_iota(I32, (T, T), 0)
        ci = lax.broadcasted_iota(I32, (T, T), 1)
        tcol = lax.broadcasted_iota(I32, (T, 1), 0)

        for t in range(Lc // T):
            cc = lax.broadcasted_iota(I32, (T, Lc), 1)
            rr = lax.broadcasted_iota(I32, (T, Lc), 0) + t * T
            band = ((cc >= rr - lo_w) & (cc < rr + hi_w)).astype(BF16)
            pos = tcol + t * T
            cnt = (jnp.minimum(pos + hi_w, Lc) - jnp.maximum(pos - lo_w, 0)).astype(F32)
            finish(band_sum(band, s_rows(slice(0, Lc))) / cnt, t * T)

        pad_scr[0:PAD, :] = jnp.zeros((PAD, pad_scr.shape[1]), F32)
        pad_scr[PAD + L:PAD + L + PAD, :] = jnp.zeros((PAD, pad_scr.shape[1]), F32)
        for p in range(s_ref.shape[0]):
            pad_scr[PAD:PAD + L, p * LANE:(p + 1) * LANE] = s_ref[p, Lc:Lc + L, :].astype(F32)
        band = ((ri // GW == ci // GW) & (ci % GW >= ri % GW - lo_w) & (ci % GW < ri % GW + hi_w)).astype(BF16)

        def tile(t, carry):
            r0 = pl.multiple_of(t * T, T)
            acc = pad_scr[pl.ds(PAD + r0 - lo_w * GW, T), :]
            for o in range(-lo_w + 1, hi_w):
                acc = acc + pad_scr[pl.ds(PAD + r0 + o * GW, T), :]
            tok = tcol + r0
            gr = tok // GW
            gc = tok % GW
            cnt_r = (jnp.minimum(gr + hi_w, NR) - jnp.maximum(gr - lo_w, 0)).astype(F32)
            cnt_c = (jnp.minimum(gc + hi_w, GW) - jnp.maximum(gc - lo_w, 0)).astype(F32)
            mean = band_sum(band, acc / cnt_r) / cnt_c
            finish(mean, Lc + r0)
            return carry

        lax.fori_loop(0, L // T, tile, 0)

    for gi, w in enumerate(windows):
        pl.when(g == gi)(functools.partial(body, w))


def _pool(P, pool_w, pool_scale, col0, B, L, Lc):
    G, cg = pool_w.shape[0], pool_w.shape[1]
    assert G == len(POOL_WINDOWS) and ROW_TILE % GRID_W == 0 and L % ROW_TILE == 0 and Lc % ROW_TILE == 0
    assert col0 % cg == 0
    RB = L + Lc
    PAD = (max(POOL_WINDOWS) // 2) * GRID_W
    return pl.pallas_call(
        functools.partial(_pool_kernel, Lc=Lc, L=L, windows=POOL_WINDOWS),
        grid=(B, G),
        in_specs=[pl.BlockSpec((cg // LANE, RB, LANE), lambda b, g: (col0 // cg + g, b, 0)),
                  pl.BlockSpec((None, cg, cg), lambda b, g: (g, 0, 0)),
                  pl.BlockSpec((1, cg), lambda b, g: (0, g))],
        out_specs=pl.BlockSpec((RB, cg), lambda b, g: (b, g)),
        out_shape=jax.ShapeDtypeStruct((B * RB, G * cg), BF16),
        scratch_shapes=[pltpu.VMEM((L + 2 * PAD, cg), F32)],
        compiler_params=_cp(2), name="pool",
    )(P, pool_w, pool_scale.reshape(1, G * cg))


def _mlstm_prep_kernel(u_ref, wg_ref, gb_ref, cq_ref, at_ref, wgt_scr, *, H):
    T = MLSTM_T

    @pl.when(pl.program_id(0) == 0)
    def _():
        wgt_scr[...] = wg_ref[...].T.astype(BF16)

    G = jnp.dot(u_ref[...], wgt_scr[...], preferred_element_type=F32) + gb_ref[...]
    ls = jax.nn.log_sigmoid(G)
    ri = lax.broadcasted_iota(I32, (T, T), 0)
    ci = lax.broadcasted_iota(I32, (T, T), 1)
    hp = lax.Precision.HIGHEST
    bf = jnp.dot((ci <= ri).astype(F32), ls, precision=hp, preferred_element_type=F32)
    bb = jnp.dot((ci >= ri).astype(F32), ls, precision=hp, preferred_element_type=F32)
    lane = lax.broadcasted_iota(I32, (T, LANE), 1)
    row = lax.broadcasted_iota(I32, (T, LANE), 0)
    fwd_lane = lane < 2 * H
    bsel = jnp.where(fwd_lane, bf, bb)
    a = G - pltpu.roll(bsel, LANE - H, 1)
    pf = a
    pb = a
    k = 1
    while k < T:
        pf = jnp.maximum(pf, jnp.where(row >= k, pltpu.roll(pf, k, 0), -jnp.inf))
        pb = jnp.maximum(pb, jnp.where(row < T - k, pltpu.roll(pb, T - k, 0), -jnp.inf))
        k *= 2
    is_i = (lane < H) | ((lane >= 2 * H) & (lane < 3 * H))
    cq_ref[...] = jnp.where(is_i, jnp.where(fwd_lane, pf, pb), bsel)
    at_ref[...] = a.T


def _mlstm_prep(u, w_gate_nk, gate_b, H):
    R, D = u.shape
    T = MLSTM_T
    gb = jnp.pad(gate_b.reshape(1, 4 * H), ((0, 0), (0, LANE - 4 * H)))
    wg = jnp.pad(w_gate_nk, ((0, LANE - 4 * H), (0, 0)))
    return pl.pallas_call(
        functools.partial(_mlstm_prep_kernel, H=H),
        grid=(R // T,),
        in_specs=[pl.BlockSpec((T, D), lambda i: (i, 0)), pl.BlockSpec((LANE, D), lambda i: (0, 0)),
                  pl.BlockSpec((1, LANE), lambda i: (0, 0))],
        out_specs=[pl.BlockSpec((T, LANE), lambda i: (i, 0)), pl.BlockSpec((None, LANE, T), lambda i: (i, 0, 0))],
        out_shape=[jax.ShapeDtypeStruct((R, LANE), F32), jax.ShapeDtypeStruct((R // T, LANE, T), F32)],
        scratch_shapes=[pltpu.VMEM((D, LANE), BF16)],
        compiler_params=_cp(1), name="mlstm_prep",
    )(u, wg, gb)


def _mlstm_kernel(q_ref, kt_ref, v_ref, og_ref, cq_ref, at_ref, ng_ref, y_ref,
                  hs_scr, c_scr, n_scr, m_scr, *, Lc, L, H, lat_only):
    T = MLSTM_T
    hh = pl.program_id(1)
    nc = (Lc + L) // T
    ncc = Lc // T
    scale = q_ref.shape[1] ** -0.5

    def wide(ref, rows):
        return jnp.concatenate([ref[p, rows, :] for p in range(ref.shape[0])], axis=1)

    ri = lax.broadcasted_iota(I32, (T, T), 0)
    ci = lax.broadcasted_iota(I32, (T, T), 1)
    ones_b = jnp.ones((T, LANE), BF16)

    for d in range(2):
        c_scr[...] = jnp.zeros(c_scr.shape, F32)
        n_scr[...] = jnp.zeros(n_scr.shape, F32)
        m_scr[...] = jnp.zeros(m_scr.shape, F32)
        tri = (ci <= ri) if d == 0 else (ci >= ri)
        end = T - 1 if d == 0 else 0

        def chunk(step, carry, d=d, tri=tri, end=end):
            if d == 0:
                c = step
            else:
                c = jnp.where(step < ncc, ncc - 1 - step, nc - 1 - (step - ncc))
            r0 = pl.multiple_of(c * T, T)
            q = q_ref[pl.ds(r0, T), :]
            kt = kt_ref[c]
            v = wide(v_ref, pl.ds(r0, T))
            cq = cq_ref[pl.ds(r0, T), :]
            pm_col = cq[:, 2 * d:2 * d + 1]
            b_col = cq[:, 2 * d + 1:2 * d + 2]
            a_row = at_ref[c, pl.ds(d * 2 * H + hh, 1), :]
            m = m_scr[...]
            mx = jnp.maximum(m, pm_col)
            dec = jnp.exp(jnp.where(tri, a_row - mx, -jnp.inf)) * scale
            s = jnp.dot(q, kt, preferred_element_type=F32)
            sw = (s * dec).astype(BF16)
            e_int = jnp.exp(m - mx) * scale
            num = jnp.dot(sw, v, preferred_element_type=F32) + e_int * jnp.dot(
                q, c_scr[...].astype(BF16), preferred_element_type=F32)
            den = (jnp.dot(sw, ones_b, preferred_element_type=F32)[:, 0:1]
                   + e_int * jnp.dot(q, n_scr[...].astype(BF16), preferred_element_type=F32)[:, 0:1])
            hout = num / jnp.maximum(jnp.abs(den), jnp.exp(-(b_col + mx)))
            if d == 0:
                hs_scr[pl.ds(r0, T), :] = hout
            else:
                hs_scr[pl.ds(r0, T), :] = hs_scr[pl.ds(r0, T), :] + hout
            mxe = jnp.maximum(m, pm_col[end:end + 1, :])
            decay = jnp.exp(m - mxe)
            kw = (kt.astype(F32) * jnp.exp(a_row - mxe)).astype(BF16)
            c_scr[...] = decay * c_scr[...] + jnp.dot(kw, v, preferred_element_type=F32)
            n_scr[...] = decay * n_scr[...] + jnp.dot(kw, ones_b, preferred_element_type=F32)
            m_scr[...] = b_col[end:end + 1, :] + mxe
            return carry

        lax.fori_loop(0, nc, chunk, 0)

    ng = ng_ref[...]
    first = ncc if lat_only else 0

    def finish(c, carry):
        r0 = pl.multiple_of(c * T, T)
        hn = _ln_rows(hs_scr[pl.ds(r0, T), :]) * ng
        y = _sigmoid(wide(og_ref, pl.ds(r0, T)).astype(F32)) * hn
        y_ref[pl.ds(pl.multiple_of(r0 - first * T, T), T), :] = y.astype(BF16)
        return carry

    lax.fori_loop(first, nc, finish, 0)


def _mlstm(QKVO, u, w_gate_nk, gate_b, norm_g, B, L, Lc, H, dqk, dv, lat_only):
    T = MLSTM_T
    RB = L + Lc
    R = B * RB
    nc = RB // T
    assert dqk % LANE == 0 and dv % LANE == 0 and L % T == 0 and Lc % T == 0 and 4 * H <= LANE
    D = H * dv
    out_rows = L if lat_only else RB
    vp = dv // dqk
    assert QKVO.shape[2] == dqk and dv % dqk == 0 and (2 * H) % vp == 0
    voff = (2 * H) // vp
    ooff = voff + H
    cq, at = _mlstm_prep(u, w_gate_nk, gate_b, H)
    cols = cq[:, :4 * H].reshape(R, 4, H).transpose(2, 0, 1)
    kt = QKVO[H:2 * H].reshape(H, R // T, T, dqk).transpose(1, 0, 3, 2).reshape(R // T, H * dqk, T)
    return pl.pallas_call(
        functools.partial(_mlstm_kernel, Lc=Lc, L=L, H=H, lat_only=lat_only),
        grid=(B, H),
        in_specs=[pl.BlockSpec((None, RB, dqk), lambda b, h: (h, b, 0)),
                  pl.BlockSpec((nc, dqk, T), lambda b, h: (b, h, 0)),
                  pl.BlockSpec((vp, RB, dqk), lambda b, h: (voff + h, b, 0)),
                  pl.BlockSpec((vp, RB, dqk), lambda b, h: (ooff + h, b, 0)),
                  pl.BlockSpec((None, RB, 4), lambda b, h: (h, b, 0)),
                  pl.BlockSpec((nc, LANE, T), lambda b, h: (b, 0, 0)),
                  pl.BlockSpec((1, dv), lambda b, h: (0, h))],
        out_specs=pl.BlockSpec((out_rows, dv), lambda b, h: (b, h)),
        out_shape=jax.ShapeDtypeStruct((B * out_rows, D), BF16),
        scratch_shapes=[pltpu.VMEM((RB, dv), F32), pltpu.VMEM((dqk, dv), F32),
                        pltpu.VMEM((dqk, LANE), F32), pltpu.VMEM((1, 1), F32)],
        compiler_params=_cp(2, VMEM_LIMIT_BIG), name="mlstm",
    )(QKVO, kt, QKVO, QKVO, cols, at, norm_g.reshape(1, D))


def _ln_mid_kernel(*refs, alpha, n_exp, tpb):
    n_res = 1 if tpb is None else 2
    res_refs = refs[:n_res]
    y_ref, mod_ref, g_ref, b_ref, wr_ref, h_out, u_out, aff_out, ub_scr, inv = refs[n_res:]
    _stage_rows(inv, [mod_ref[2:3, :], g_ref[...], b_ref[...], 1.0 + mod_ref[4:5, :], mod_ref[3:4, :]])

    def go(residual):
        def blk(r0):
            rows16 = pl.ds(r0, LN_ROWS)
            y = y_ref[rows16, :].astype(F32)
            us = []
            for k, rows in enumerate(_halves(r0)):
                v = alpha * residual(rows) + inv[0] * y[k * SUBLANE:(k + 1) * SUBLANE]
                h1 = _ln_rows(v) * inv[1] + inv[2]
                h_out[rows, :] = h1
                us.append(h1 * inv[3] + inv[4])
            u2 = jnp.concatenate(us, axis=0)
            ub_scr[rows16, :] = u2.astype(BF16)
            u_out[rows16, :] = u2
        _row_blocks(ROW_TILE, blk)

    if tpb is None:
        go(lambda rows: res_refs[0][rows, :])
    else:
        is_ctx = (pl.program_id(0) % tpb) == 0
        pl.when(is_ctx)(lambda: go(lambda rows: _ln_rows(res_refs[1][rows, :])))
        pl.when(jnp.logical_not(is_ctx))(lambda: go(lambda rows: _ln_rows(res_refs[0][rows, :])))

    logits = jnp.dot(ub_scr[...], wr_ref[...], preferred_element_type=F32)
    lane = lax.broadcasted_iota(I32, logits.shape, 1)
    logits = jnp.where(lane < n_exp, logits, -jnp.inf)
    e = jnp.exp(logits - jnp.max(logits, axis=-1, keepdims=True))
    aff_out[...] = e / jnp.sum(e, axis=-1, keepdims=True)


def _ln_mid(res, y, mods, layer, ln_g, ln_b, w_router, alpha, B, L, Lc, lat_only):
    raw = isinstance(res, tuple)
    D = y.shape[-1]
    E = w_router.shape[1]
    tpb = (L + Lc) // ROW_TILE
    lt = L // ROW_TILE
    wr = jnp.pad(w_router, ((0, 0), (0, LANE - E))).astype(BF16)
    if lat_only:
        assert not raw
        n_tiles = B * lt
        res_specs = [pl.BlockSpec((ROW_TILE, D), lambda i: (i + 1 + i // lt, 0))]
        mod_map = lambda i: (layer, i // lt, 0, 0)
    else:
        n_tiles = B * tpb
        mod_map = lambda i: (layer, jnp.where(i % tpb == 0, B, i // tpb), 0, 0)
        if raw:
            res_specs = [pl.BlockSpec((ROW_TILE, D), lambda i: ((i // tpb) * lt + jnp.maximum(i % tpb - 1, 0), 0)),
                         pl.BlockSpec((ROW_TILE, D), lambda i: (i // tpb, 0))]
        else:
            res_specs = [pl.BlockSpec((ROW_TILE, D), lambda i: (i, 0))]
    rows = n_tiles * ROW_TILE
    res_args = res if raw else (res,)
    return pl.pallas_call(
        functools.partial(_ln_mid_kernel, alpha=alpha, n_exp=E, tpb=tpb if raw else None),
        grid=(n_tiles,),
        in_specs=res_specs
        + [pl.BlockSpec((ROW_TILE, D), lambda i: (i, 0)),
                  pl.BlockSpec((None, None, 6, D), mod_map),
                  pl.BlockSpec((1, D), lambda i: (0, 0)),
                  pl.BlockSpec((1, D), lambda i: (0, 0)),
                  pl.BlockSpec((D, LANE), lambda i: (0, 0))],
        out_specs=[pl.BlockSpec((ROW_TILE, D), lambda i: (i, 0)),
                   pl.BlockSpec((ROW_TILE, D), lambda i: (i, 0)),
                   pl.BlockSpec((ROW_TILE, LANE), lambda i: (i, 0))],
        out_shape=[jax.ShapeDtypeStruct((rows, D), F32), jax.ShapeDtypeStruct((rows, D), F32),
                   jax.ShapeDtypeStruct((rows, LANE), F32)],
        scratch_shapes=[pltpu.VMEM((ROW_TILE, D), BF16), pltpu.VMEM((5, SUBLANE, D), F32)],
        compiler_params=_cp(1), name="ln_mid",
    )(*res_args, y, mods, ln_g.reshape(1, D), ln_b.reshape(1, D), wr)


def _moe_gu_kernel(rows_ref, u_hbm, wg_ref, wu_ref, o_ref, xg, xb, sem, *, ns, nf_static):
    e = pl.program_id(0)
    f = pl.program_id(1)
    ne = pl.num_programs(0)
    nf = pl.num_programs(1)
    per = ns // nf_static

    def row_copy(table_row, dst_row):
        r = rows_ref[table_row]
        return pltpu.make_async_copy(u_hbm.at[pl.ds(r, 1), :], xg.at[pl.ds(dst_row, 1), :], sem.at[0])

    def wait_all():
        pltpu.make_async_copy(u_hbm.at[pl.ds(0, ns), :], xg, sem.at[0]).wait()

    @pl.when((e == 0) & (f == 0))
    def _():
        def body(s, carry):
            row_copy(s, s).start()
            return carry
        lax.fori_loop(0, ns, body, 0, unroll=8)

    @pl.when(f == 0)
    def _():
        wait_all()
        xb[...] = xg[...].astype(BF16)

    first = (e + 1) * ns + f * per
    for s in range(per):
        row_copy(first + s, f * per + s).start()

    x = xb[...]
    hg = jnp.dot(x, wg_ref[...].astype(BF16), preferred_element_type=F32)
    hu = jnp.dot(x, wu_ref[...].astype(BF16), preferred_element_type=F32)
    o_ref[...] = (hg * _sigmoid(hg) * hu).astype(BF16)

    @pl.when((e == ne - 1) & (f == nf - 1))
    def _():
        wait_all()


def _moe_gu(rows, u2p, w_gu, layer):
    _, E, D, F2 = w_gu.shape
    F = F2 // 2
    ns = rows.shape[1]
    fc = min(MOE_FC, F)
    nf = F // fc
    assert ns % 8 == 0 and ns % nf == 0
    rows = jnp.concatenate([rows, rows[:1]], axis=0)
    grid_spec = pltpu.PrefetchScalarGridSpec(
        num_scalar_prefetch=1,
        grid=(E, nf),
        in_specs=[pl.BlockSpec(memory_space=pl.ANY),
                  pl.BlockSpec((None, None, D, fc), lambda e, f, rows: (layer, e, 0, f)),
                  pl.BlockSpec((None, None, D, fc), lambda e, f, rows: (layer, e, 0, nf + f))],
        out_specs=pl.BlockSpec((None, ns, fc), lambda e, f, rows: (e, 0, f)),
        scratch_shapes=[pltpu.VMEM((ns, D), F32), pltpu.VMEM((ns, D), BF16),
                        pltpu.SemaphoreType.DMA((1,))],
    )
    return pl.pallas_call(
        functools.partial(_moe_gu_kernel, ns=ns, nf_static=nf),
        grid_spec=grid_spec,
        out_shape=jax.ShapeDtypeStruct((E, ns, F), BF16),
        compiler_params=_cp(2), name="moe_gu",
    )(rows.reshape(-1), u2p, w_gu, w_gu)


def _moe_down_kernel(h_ref, w_ref, o_ref):
    o_ref[...] = jnp.dot(h_ref[...], w_ref[...].astype(BF16), preferred_element_type=F32).astype(BF16)


def _moe_down(hmid, w_down, layer):
    _, E, F, D = w_down.shape
    ns = hmid.shape[1]
    tn = min(MOE_TN, D)
    return pl.pallas_call(
        _moe_down_kernel,
        grid=(E, D // tn),
        in_specs=[pl.BlockSpec((None, ns, F), lambda e, j: (e, 0, 0)),
                  pl.BlockSpec((None, None, F, tn), lambda e, j: (layer, e, 0, j))],
        out_specs=pl.BlockSpec((None, ns, tn), lambda e, j: (e, 0, j)),
        out_shape=jax.ShapeDtypeStruct((E, ns, D), BF16),
        compiler_params=_cp(2), name="moe_down",
    )(hmid, w_down)


def _combine_kernel(a_ref, np_ref, y_hbm, pos_ref, aff_ref, av_ref, h_ref, mod_ref, modn_ref, g_ref, b_ref,
                    h_out, u_out, ybuf, acc, inv, sem, *, n_exp, ns, alpha, final):
    i = pl.program_id(0)
    n_tiles = pl.num_programs(0)
    slot = i % 2
    W = COMB_W
    pos = pos_ref[...]
    a_v = av_ref[...]
    gate = aff_ref[...].astype(BF16)
    sub = lax.broadcasted_iota(I32, (LANE, n_exp * W), 0)
    lan = lax.broadcasted_iota(I32, (LANE, n_exp * W), 1)
    expand = (lan // W == sub).astype(BF16)
    kk = (lax.broadcasted_iota(I32, (1, n_exp * W), 1) % W).astype(F32)

    def strip(tile, p, e, sl):
        w0 = pl.multiple_of(jnp.minimum(a_ref[tile * n_exp + e] + W * p, ns - W), COMB_ALIGN)
        return pltpu.make_async_copy(y_hbm.at[e, pl.ds(w0, W), :], ybuf.at[sl, pl.ds(e * W, W), :], sem.at[sl])

    def fetch(tile, p, sl):
        for e in range(n_exp):
            strip(tile, p, e, sl).start()

    def land(tile, p, sl):
        for e in range(n_exp):
            strip(tile, p, e, sl).wait()

    gate_x = jnp.dot(gate, expand, preferred_element_type=F32)

    def selector(p):
        start = a_v + W * p
        valid = (pos >= start) & (pos < start + W)
        rel = jnp.where(valid, pos - jnp.minimum(start, ns - W), -1).astype(F32).astype(BF16)
        relx = jnp.dot(rel, expand, preferred_element_type=F32)
        return jnp.where(relx == kk, gate_x, 0.0).astype(BF16)

    @pl.when(i == 0)
    def _():
        fetch(0, 0, 0)

    @pl.when(i + 1 < n_tiles)
    def _():
        fetch(i + 1, 0, 1 - slot)

    sel0 = selector(0)
    land(i, 0, slot)
    acc[...] = jnp.dot(sel0, ybuf[slot], preferred_element_type=F32)

    def extra_pass(p, carry):
        fetch(i, p, slot)
        selp = selector(p)
        land(i, p, slot)
        acc[...] += jnp.dot(selp, ybuf[slot], preferred_element_type=F32)
        return carry

    lax.fori_loop(1, np_ref[i], extra_pass, 0)

    vecs = [mod_ref[5:6, :], g_ref[...], b_ref[...]]
    if not final:
        vecs += [1.0 + modn_ref[1:2, :], modn_ref[0:1, :]]
    _stage_rows(inv, vecs)

    def blk(r0):
        us = []
        for rows in _halves(r0):
            v = alpha * h_ref[rows, :] + inv[0] * acc[rows, :]
            h2 = _ln_rows(v) * inv[1] + inv[2]
            h_out[rows, :] = h2
            if not final:
                us.append(h2 * inv[3] + inv[4])
        if not final:
            u_out[pl.ds(r0, LN_ROWS), :] = jnp.concatenate(us, axis=0).astype(BF16)

    _row_blocks(ROW_TILE, blk)


def _combine(Y, pos, aff, a_tab, npass, h1, mods, layer, ln_g, ln_b, alpha, B, L, Lc, lat_only, final):
    E, ns, D = Y.shape
    tpb = (L + Lc) // ROW_TILE
    lt = L // ROW_TILE
    n_tiles = h1.shape[0] // ROW_TILE
    nxt = layer if final else layer + 1
    if lat_only:
        sel = lambda i: i // lt
    else:
        sel = lambda i: jnp.where(i % tpb == 0, B, i // tpb)
    a_pad = a_tab.reshape(n_tiles, 1, LANE)
    out_shape = [jax.ShapeDtypeStruct((n_tiles * ROW_TILE, D), F32)]
    out_specs = [pl.BlockSpec((ROW_TILE, D), lambda i, a, n: (i, 0))]
    if not final:
        out_shape.append(jax.ShapeDtypeStruct((n_tiles * ROW_TILE, D), BF16))
        out_specs.append(pl.BlockSpec((ROW_TILE, D), lambda i, a, n: (i, 0)))
    grid_spec = pltpu.PrefetchScalarGridSpec(
        num_scalar_prefetch=2,
        grid=(n_tiles,),
        in_specs=[pl.BlockSpec(memory_space=pl.ANY),
                  pl.BlockSpec((ROW_TILE, LANE), lambda i, a, n: (i, 0)),
                  pl.BlockSpec((ROW_TILE, LANE), lambda i, a, n: (i, 0)),
                  pl.BlockSpec((None, 1, LANE), lambda i, a, n: (i, 0, 0)),
                  pl.BlockSpec((ROW_TILE, D), lambda i, a, n: (i, 0)),
                  pl.BlockSpec((None, None, 6, D), lambda i, a, n: (layer, sel(i), 0, 0)),
                  pl.BlockSpec((None, None, 6, D), lambda i, a, n: (nxt, sel(i), 0, 0)),
                  pl.BlockSpec((1, D), lambda i, a, n: (0, 0)),
                  pl.BlockSpec((1, D), lambda i, a, n: (0, 0))],
        out_specs=out_specs,
        scratch_shapes=[pltpu.VMEM((2, E * COMB_W, D), BF16), pltpu.VMEM((ROW_TILE, D), F32),
                        pltpu.VMEM((3 if final else 5, SUBLANE, D), F32), pltpu.SemaphoreType.DMA((2,))],
    )

    def kern(*refs):
        if final:
            _combine_kernel(*refs[:12], None, *refs[12:], n_exp=E, ns=ns, alpha=alpha, final=True)
        else:
            _combine_kernel(*refs, n_exp=E, ns=ns, alpha=alpha, final=False)

    return pl.pallas_call(
        kern, grid_spec=grid_spec, out_shape=out_shape,
        compiler_params=_cp(1), name="moe_combine",
    )(a_tab[:, :E].reshape(-1), npass, Y, pos, aff, a_pad, h1, mods, mods, ln_g.reshape(1, D), ln_b.reshape(1, D))


def _route(aff, E, B, L, Lc, lat_only):
    RB = L if lat_only else L + Lc
    n_rows = B * RB
    a3 = aff[:, :E].reshape(B, RB, E)

    def pick(a, n, off, slot0):
        cap = CAPACITY_FACTOR * n // E
        at = jnp.swapaxes(a, 1, 2)
        top, idx = lax.top_k(at, cap)
        idx = jnp.sort(idx, axis=-1)
        rows = idx + off + (jnp.arange(B, dtype=I32) * RB)[:, None, None]
        thr = top[..., cap - 1:cap]
        gt = at > thr
        eq = at == thr
        need = cap - jnp.sum(gt, axis=-1, keepdims=True)
        sel = gt | (eq & (jnp.cumsum(eq, axis=-1) <= need))
        slot = jnp.cumsum(sel, axis=-1) - 1 + slot0 + (jnp.arange(B, dtype=I32) * cap)[:, None, None]
        pos = jnp.swapaxes(jnp.where(sel, slot, -1), 1, 2).astype(I32)
        return jnp.swapaxes(rows, 0, 1).reshape(E, B * cap), pos

    if lat_only:
        rows, pos = pick(a3, L, 0, 0)
    else:
        r_l, p_l = pick(a3[:, Lc:], L, Lc, 0)
        r_c, p_c = pick(a3[:, :Lc], Lc, 0, r_l.shape[1])
        rows = jnp.concatenate([r_l, r_c], axis=1)
        pos = jnp.concatenate([p_c, p_l], axis=1)
    ns = rows.shape[1]
    assert ns % COMB_ALIGN == 0 and ns >= COMB_W
    pos = jnp.pad(pos.reshape(n_rows, E), ((0, 0), (0, LANE - E)), constant_values=-1)
    pt = pos.reshape(n_rows // ROW_TILE, ROW_TILE, LANE)
    has = pt >= 0
    cnt = jnp.sum(has, axis=1)
    lo = jnp.min(jnp.where(has, pt, ns), axis=1)
    lo = jnp.where(cnt > 0, lo, 0)
    a_tab = (lo // COMB_ALIGN) * COMB_ALIGN
    span = lo + cnt - a_tab
    npass = jnp.maximum(jnp.max((span + COMB_W - 1) // COMB_W, axis=1), 1).astype(I32)
    return rows.astype(I32), pos, a_tab.astype(I32), npass


def kernel(x, c, ctx, c_ctx, ada_w, ada_b, ln_g, ln_b, even_w_in, even_conv_w, even_conv_b, lru_gate_w,
           lru_gate_b, lru_lambda, pool_w, pool_scale, even_w_out, odd_w_in, odd_gate_b, odd_norm_g,
           odd_w_out, router_w, expert_w_gu, expert_w_down):
    B, L, D = x.shape
    Lc = ctx.shape[1]
    depth = ada_w.shape[0]
    E = router_w.shape[2]
    assert Lc == ROW_TILE and L % ROW_TILE == 0 and B + 1 <= 8 and depth >= 2
    alpha = (2 * depth) ** 0.25

    cond8 = jnp.zeros((8, D), F32).at[:B].set(c).at[B].set(c_ctx)
    mods = _adaln(cond8, ada_w, ada_b).reshape(depth, 8, 6, D)

    h = (x.reshape(B * L, D), ctx.reshape(B * Lc, D))
    u = _ln_in(*h, mods, B, L, Lc)

    for layer in range(depth):
        j = layer // 2
        last = layer == depth - 1
        if layer % 2 == 0:
            d_lru = lru_gate_w.shape[3] * lru_gate_w.shape[4]
            P = _matmul([u], even_w_in, j, even_w_in.shape[2], BF16, piece=LANE, name="even_w_in")
            mix_lru = _rglru(P, even_conv_w[j], even_conv_b[j], lru_gate_w[j], lru_gate_b[j], lru_lambda[j], B, L, Lc)
            mix_pool = _pool(P, pool_w[j], pool_scale[j], 2 * d_lru, B, L, Lc)
            if last:
                keep = lambda t: t.reshape(B, L + Lc, -1)[:, Lc:].reshape(B * L, -1)
                mix_lru, mix_pool = keep(mix_lru), keep(mix_pool)
            y = _matmul([mix_lru, mix_pool], even_w_out, j, D, BF16, name="even_w_out")
        else:
            H = odd_gate_b.shape[1] // 4
            dv = D // H
            dqk = dv // 2
            n_main = 2 * H * dqk + 2 * D
            w_in_t = jnp.swapaxes(odd_w_in, 1, 2)
            QKVO = _matmul([u], w_in_t, j, n_main, BF16, w_is_nk=True, piece=dqk, name="odd_w_in")
            hm = _mlstm(QKVO, u, w_in_t[j, n_main:, :], odd_gate_b[j], odd_norm_g[j], B, L, Lc, H, dqk, dv, last)
            y = _matmul([hm], odd_w_out, j, D, BF16, name="odd_w_out")

        h1, u2p, aff = _ln_mid(h, y, mods, layer, ln_g[layer, 0], ln_b[layer, 0], router_w[layer], alpha,
                               B, L, Lc, last)
        rows, pos, a_tab, npass = _route(aff, E, B, L, Lc, last)
        hmid = _moe_gu(rows, u2p, expert_w_gu, layer)
        Y = _moe_down(hmid, expert_w_down, layer)
        outs = _combine(Y, pos, aff, a_tab, npass, h1, mods, layer, ln_g[layer, 1], ln_b[layer, 1], alpha,
                        B, L, Lc, last, last)
        if last:
            return outs[0].reshape(B, L, D)
        h, u = outs
```

```python
import functools

import jax
import jax.numpy as jnp
from jax import lax
from jax.experimental import pallas as pl
from jax.experimental.pallas import tpu as pltpu

F32 = jnp.float32
BF16 = jnp.bfloat16
I32 = jnp.int32

LN_EPS = 1e-5
TINY = 1e-30
GRID_W = 64
LRU_C = 8.0
POOL_WINDOWS = (2, 4, 8, 16)
CAPACITY_FACTOR = 2

LANE = 128
SUBLANE = 8
LN_ROWS = 64
ROW_TILE = 256
MM_TM = 512
MM_TN = 1024
MM_KC = 512
LRU_HEADS_PER_STEP = 2
SCAN_UNROLL = 8
MLSTM_T = 256
MOE_FC = 256
MOE_TN = 2048
COMB_W_SHIFT = 6
COMB_W = 1 << COMB_W_SHIFT
COMB_ALIGN_SHIFT = 4
COMB_ALIGN = 1 << COMB_ALIGN_SHIFT
VMEM_LIMIT = 56 * 1024 * 1024
VMEM_LIMIT_BIG = 60 * 1024 * 1024


def _cp(n_axes, vmem=VMEM_LIMIT):
    return pltpu.CompilerParams(dimension_semantics=("arbitrary",) * n_axes, vmem_limit_bytes=vmem)


def _ln_rows(v):
    mu = jnp.mean(v, axis=-1, keepdims=True)
    d = v - mu
    var = jnp.mean(d * d, axis=-1, keepdims=True)
    return d * lax.rsqrt(var + LN_EPS)


def _sigmoid(x):
    return 0.5 + 0.5 * jnp.tanh(0.5 * x)


def _stage_rows(scr, vecs):
    for k, v in enumerate(vecs):
        scr[k] = jnp.broadcast_to(v, scr.shape[1:])


def _row_blocks(n_rows, body):
    def step(i, carry):
        body(pl.multiple_of(i * LN_ROWS, LN_ROWS))
        return carry
    lax.fori_loop(0, n_rows // LN_ROWS, step, 0)


def _halves(r0):
    return [pl.ds(r0 + k * SUBLANE, SUBLANE) for k in range(LN_ROWS // SUBLANE)]


def _adaln_kernel(c_ref, w_ref, b_ref, o_ref):
    c = c_ref[...]
    s = (c * _sigmoid(c)).astype(BF16)
    o_ref[...] = jnp.dot(s, w_ref[...].astype(BF16), preferred_element_type=F32) + b_ref[...]


def _adaln(cond8, ada_w, ada_b):
    depth, D, N = ada_w.shape
    tn = 512
    return pl.pallas_call(
        _adaln_kernel,
        grid=(depth, N // tn),
        in_specs=[pl.BlockSpec((8, D), lambda l, j: (0, 0)),
                  pl.BlockSpec((None, D, tn), lambda l, j: (l, 0, j)),
                  pl.BlockSpec((None, 1, tn), lambda l, j: (l, 0, j))],
        out_specs=pl.BlockSpec((None, 8, tn), lambda l, j: (l, 0, j)),
        out_shape=jax.ShapeDtypeStruct((depth, 8, N), F32),
        compiler_params=_cp(2), name="adaln",
    )(cond8, ada_w, ada_b.reshape(depth, 1, N))


def _ln_in_kernel(xl_ref, xc_ref, mod_ref, u_ref, inv, *, tpb):
    is_ctx = (pl.program_id(0) % tpb) == 0
    _stage_rows(inv, [1.0 + mod_ref[1:2, :], mod_ref[0:1, :]])

    def go(src):
        def blk(r0):
            us = [_ln_rows(src[rows, :]) * inv[0] + inv[1] for rows in _halves(r0)]
            u_ref[pl.ds(r0, LN_ROWS), :] = jnp.concatenate(us, axis=0).astype(BF16)
        _row_blocks(ROW_TILE, blk)

    pl.when(is_ctx)(lambda: go(xc_ref))
    pl.when(jnp.logical_not(is_ctx))(lambda: go(xl_ref))


def _ln_in(xl, xc, mods, B, L, Lc):
    D = xl.shape[-1]
    tpb = (L + Lc) // ROW_TILE
    lt = L // ROW_TILE
    R = B * (L + Lc)
    return pl.pallas_call(
        functools.partial(_ln_in_kernel, tpb=tpb),
        grid=(B * tpb,),
        in_specs=[pl.BlockSpec((ROW_TILE, D), lambda i: ((i // tpb) * lt + jnp.maximum(i % tpb - 1, 0), 0)),
                  pl.BlockSpec((ROW_TILE, D), lambda i: (i // tpb, 0)),
                  pl.BlockSpec((None, None, 6, D), lambda i: (0, jnp.where(i % tpb == 0, B, i // tpb), 0, 0))],
        out_specs=pl.BlockSpec((ROW_TILE, D), lambda i: (i, 0)),
        out_shape=jax.ShapeDtypeStruct((R, D), BF16),
        scratch_shapes=[pltpu.VMEM((2, SUBLANE, D), F32)],
        compiler_params=_cp(1), name="ln_in",
    )(xl, xc, mods)


def _mm_kernel(*refs, ksplits, w_is_nk, piece):
    nx = len(ksplits)
    x_refs, w_ref, o_ref, wb_ref = refs[:nx], refs[nx], refs[nx + 1], refs[nx + 2]

    @pl.when(pl.program_id(1) == 0)
    def _():
        if w_is_nk:
            for k0 in range(0, wb_ref.shape[0], MM_KC):
                wb_ref[k0:k0 + MM_KC, :] = w_ref[:, k0:k0 + MM_KC].T.astype(BF16)
        else:
            wb_ref[...] = w_ref[...].astype(BF16)

    acc = None
    for x_ref, (k0, k1) in zip(x_refs, ksplits):
        p = jnp.dot(x_ref[...], wb_ref[k0:k1, :], preferred_element_type=F32)
        acc = p if acc is None else acc + p
    if piece is None:
        o_ref[...] = acc.astype(o_ref.dtype)
    else:
        for c in range(o_ref.shape[0]):
            o_ref[c] = acc[:, c * piece:(c + 1) * piece].astype(o_ref.dtype)


def _matmul(xs, w, widx, n_cols, out_dtype, w_is_nk=False, piece=None, tm=MM_TM, name="mm"):
    R = xs[0].shape[0]
    K = w.shape[2] if w_is_nk else w.shape[1]
    tn = next(t for t in (MM_TN, MM_TN // 2, MM_TN // 4, LANE) if n_cols % t == 0)
    ksplits, k0 = [], 0
    for x in xs:
        ksplits.append((k0, k0 + x.shape[1]))
        k0 += x.shape[1]
    assert k0 == K and R % tm == 0 and K % MM_KC == 0
    if w_is_nk:
        w_spec = pl.BlockSpec((None, tn, K), lambda j, i: (widx, j, 0))
    else:
        w_spec = pl.BlockSpec((None, K, tn), lambda j, i: (widx, 0, j))
    if piece is None:
        out_spec = pl.BlockSpec((tm, tn), lambda j, i: (i, j))
        out_shape = jax.ShapeDtypeStruct((R, n_cols), out_dtype)
    else:
        assert tn % piece == 0
        out_spec = pl.BlockSpec((tn // piece, tm, piece), lambda j, i: (j, i, 0))
        out_shape = jax.ShapeDtypeStruct((n_cols // piece, R, piece), out_dtype)
    return pl.pallas_call(
        functools.partial(_mm_kernel, ksplits=tuple(ksplits), w_is_nk=w_is_nk, piece=piece),
        grid=(n_cols // tn, R // tm),
        in_specs=[pl.BlockSpec((tm, x.shape[1]), lambda j, i: (i, 0)) for x in xs] + [w_spec],
        out_specs=out_spec,
        out_shape=out_shape,
        scratch_shapes=[pltpu.VMEM((K, tn), BF16)],
        compiler_params=_cp(2, VMEM_LIMIT_BIG), name=name,
    )(*xs, w)


def _scan_pitch(seg):
    return seg + SUBLANE


def _rglru_kernel(x_ref, z_ref, cw_ref, cb_ref, gw_ref, gb_ref, lam_ref, o_ref,
                  xs_scr, a_scr, b_scr, o_scr, *, Lc, L, nh):
    cw = cw_ref[...]
    cb = cb_ref[...]
    gb_half = 0.5 * gb_ref[...]
    gw_half = [[[(0.5 * gw_ref[d, g, hd]).astype(BF16) for hd in range(nh)] for g in range(2)] for d in range(2)]
    half_c_ls = (0.5 * LRU_C) * jax.nn.log_sigmoid(lam_ref[...])
    width = nh * LANE
    pad = SUBLANE
    off_c, off_l = pad, 2 * pad + Lc
    sequences = ((0, Lc, 0, off_c), (Lc, L, SUBLANE * _scan_pitch(Lc // SUBLANE), off_l))

    zpad = jnp.zeros((pad, width), F32)
    xs_scr[0:pad, :] = zpad
    xs_scr[off_c + Lc:off_c + Lc + pad, :] = zpad
    xs_scr[off_l + L:off_l + L + pad, :] = zpad
    for hd in range(nh):
        xs_scr[off_c:off_c + Lc, hd * LANE:(hd + 1) * LANE] = x_ref[hd, 0:Lc, :].astype(F32)
        xs_scr[off_l:off_l + L, hd * LANE:(hd + 1) * LANE] = x_ref[hd, Lc:Lc + L, :].astype(F32)

    for r0, n, base, off in sequences:
        seg = n // SUBLANE
        pitch = _scan_pitch(seg)
        for j in range(SUBLANE):
            s0 = off + j * seg
            xc = (cw[0:1] * xs_scr[s0 - 1:s0 - 1 + seg, :] + cw[1:2] * xs_scr[s0:s0 + seg, :]
                  + cw[2:3] * xs_scr[s0 + 1:s0 + 1 + seg, :] + cw[3:4] * xs_scr[s0 + 2:s0 + 2 + seg, :] + cb)
            xb = xc.astype(BF16)
            for hd in range(nh):
                cols = slice(hd * LANE, (hd + 1) * LANE)
                xh = xc[:, cols]
                xbh = xb[:, cols]
                for d in range(2):
                    tr = jnp.tanh(jnp.dot(xbh, gw_half[d][0][hd], preferred_element_type=F32) + gb_half[2 * d:2 * d + 1, cols])
                    ti = jnp.tanh(jnp.dot(xbh, gw_half[d][1][hd], preferred_element_type=F32) + gb_half[2 * d + 1:2 * d + 2, cols])
                    log_a = half_c_ls[d:d + 1, cols] * (1.0 + tr)
                    th = jnp.tanh(log_a)
                    p = -2.0 * th
                    mult = p * lax.rsqrt(jnp.maximum(p * (1.0 - th), TINY))
                    bx = mult * ((0.5 + 0.5 * ti) * xh)
                    rows = slice(base + j * pitch, base + j * pitch + seg)
                    a_scr[d * nh + hd, rows, :] = jnp.exp(log_a)
                    b_scr[d * nh + hd, rows, :] = bx

    rid = lax.broadcasted_iota(I32, (SUBLANE, LANE), 0)

    def carries(H, A, init, reverse):
        c = init
        cv = jnp.broadcast_to(init, (SUBLANE, LANE))
        order = range(SUBLANE - 1, -1, -1) if reverse else range(SUBLANE)
        for j in order:
            cv = jnp.where(rid == j, c, cv)
            c = A[j:j + 1] * c + H[j:j + 1]
        return cv, c

    zero = jnp.zeros((SUBLANE, LANE), F32)
    one = jnp.ones((SUBLANE, LANE), F32)
    state = [jnp.zeros((1, LANE), F32)] * (2 * nh)
    for r0, n, base, off in sequences:
        seg = n // SUBLANE
        pitch = _scan_pitch(seg)

        def rows_at(k, t, base=base, seg=seg, pitch=pitch):
            return pl.ds(base + (t if k < nh else seg - 1 - t), SUBLANE, stride=pitch)

        def pair(k, i):
            a0 = a_scr[k, rows_at(k, 2 * i), :]
            b0 = b_scr[k, rows_at(k, 2 * i), :]
            a1 = a_scr[k, rows_at(k, 2 * i + 1), :]
            return a0, b0, a1 * a0, a1 * b0 + b_scr[k, rows_at(k, 2 * i + 1), :]

        def ends(i, c):
            out = []
            for k in range(2 * nh):
                _, _, a2, b2 = pair(k, i)
                out += [a2 * c[2 * k] + b2, a2 * c[2 * k + 1]]
            return tuple(out)

        fin = lax.fori_loop(0, seg // 2, ends, (zero, one) * (2 * nh), unroll=SCAN_UNROLL)
        cvs = []
        for k in range(2 * nh):
            cv, state[k] = carries(fin[2 * k], fin[2 * k + 1], state[k], k >= nh)
            cvs.append(cv)

        def full(i, c):
            out = []
            for k in range(2 * nh):
                a0, b0, a2, b2 = pair(k, i)
                o_scr[k, rows_at(k, 2 * i), :] = a0 * c[k] + b0
                h = a2 * c[k] + b2
                o_scr[k, rows_at(k, 2 * i + 1), :] = h
                out.append(h)
            return tuple(out)

        lax.fori_loop(0, seg // 2, full, tuple(cvs), unroll=SCAN_UNROLL)

        for j in range(SUBLANE):
            rows = slice(base + j * pitch, base + j * pitch + seg)
            orow = slice(r0 + j * seg, r0 + (j + 1) * seg)
            for hd in range(nh):
                h = o_scr[hd, rows, :] + o_scr[nh + hd, rows, :]
                g = jax.nn.gelu(z_ref[hd, orow, :].astype(F32))
                o_ref[orow, hd * LANE:(hd + 1) * LANE] = (h * g).astype(BF16)


def _rglru(P, conv_w, conv_b, gate_w, gate_b, lam, B, L, Lc):
    H, blk = gate_w.shape[2], gate_w.shape[3]
    d_lru = H * blk
    nh = LRU_HEADS_PER_STEP if H % LRU_HEADS_PER_STEP == 0 else 1
    wid = nh * blk
    assert blk == LANE and Lc % (SUBLANE * SUBLANE) == 0 and L % (SUBLANE * SUBLANE) == 0
    RB = L + Lc
    scr_rows = SUBLANE * (_scan_pitch(Lc // SUBLANE) + _scan_pitch(L // SUBLANE))
    zoff = H // nh
    assert P.shape[2] == blk
    return pl.pallas_call(
        functools.partial(_rglru_kernel, Lc=Lc, L=L, nh=nh),
        grid=(B, H // nh),
        in_specs=[pl.BlockSpec((nh, RB, blk), lambda b, h: (h, b, 0)),
                  pl.BlockSpec((nh, RB, blk), lambda b, h: (zoff + h, b, 0), pipeline_mode=pl.Buffered(1)),
                  pl.BlockSpec((conv_w.shape[0], wid), lambda b, h: (0, h)),
                  pl.BlockSpec((1, wid), lambda b, h: (0, h)),
                  pl.BlockSpec((2, 2, nh, blk, blk), lambda b, h: (0, 0, h, 0, 0)),
                  pl.BlockSpec((4, wid), lambda b, h: (0, h)),
                  pl.BlockSpec((2, wid), lambda b, h: (0, h))],
        out_specs=pl.BlockSpec((RB, wid), lambda b, h: (b, h)),
        out_shape=jax.ShapeDtypeStruct((B * RB, d_lru), BF16),
        scratch_shapes=[pltpu.VMEM((RB + 3 * SUBLANE, wid), F32)]
        + [pltpu.VMEM((2 * nh, scr_rows, LANE), F32)] * 3,
        compiler_params=_cp(2, VMEM_LIMIT_BIG), name="rglru",
    )(P, P, conv_w, conv_b.reshape(1, d_lru), gate_w, gate_b.reshape(4, d_lru), lam)


def _pool_kernel(s_ref, pw_ref, ps_ref, o_ref, pad_scr, *, Lc, L, windows):
    g = pl.program_id(1)
    GW = GRID_W
    NR = L // GW
    PAD = (max(windows) // 2) * GW
    T = ROW_TILE
    pw = pw_ref[...].astype(BF16)
    scale = ps_ref[...]

    def s_rows(rows):
        return jnp.concatenate([s_ref[p, rows, :] for p in range(s_ref.shape[0])], axis=1).astype(F32)

    def finish(mean, r0):
        d = (mean - s_rows(pl.ds(r0, T))).astype(BF16)
        y = jnp.dot(d, pw, preferred_element_type=F32) * scale
        o_ref[pl.ds(r0, T), :] = y.astype(BF16)

    def band_sum(band, v):
        hi = v.astype(BF16)
        lo = (v - hi.astype(F32)).astype(BF16)
        return jnp.dot(band, hi, preferred_element_type=F32) + jnp.dot(band, lo, preferred_element_type=F32)

    def body(w):
        lo_w, hi_w = w // 2, w - w // 2
        ri = lax.broadcasted_iota(I32, (T, T), 0)
        ci = lax.broadcasted_iota(I32, (T, T), 1)
        tcol = lax.broadcasted_iota(I32, (T, 1), 0)

        for t in range(Lc // T):
            cc = lax.broadcasted_iota(I32, (T, Lc), 1)
            rr = lax.broadcasted_iota(I32, (T, Lc), 0) + t * T
            band = ((cc >= rr - lo_w) & (cc < rr + hi_w)).astype(BF16)
            pos = tcol + t * T
            cnt = (jnp.minimum(pos + hi_w, Lc) - jnp.maximum(pos - lo_w, 0)).astype(F32)
            finish(band_sum(band, s_rows(slice(0, Lc))) / cnt, t * T)

        pad_scr[0:PAD, :] = jnp.zeros((PAD, pad_scr.shape[1]), F32)
        pad_scr[PAD + L:PAD + L + PAD, :] = jnp.zeros((PAD, pad_scr.shape[1]), F32)
        for p in range(s_ref.shape[0]):
            pad_scr[PAD:PAD + L, p * LANE:(p + 1) * LANE] = s_ref[p, Lc:Lc + L, :].astype(F32)
        band = ((ri // GW == ci // GW) & (ci % GW >= ri % GW - lo_w) & (ci % GW < ri % GW + hi_w)).astype(BF16)

        def tile(t, carry):
            r0 = pl.multiple_of(t * T, T)
            acc = pad_scr[pl.ds(PAD + r0 - lo_w * GW, T), :]
            for o in range(-lo_w + 1, hi_w):
                acc = acc + pad_scr[pl.ds(PAD + r0 + o * GW, T), :]
            tok = tcol + r0
            gr = tok // GW
            gc = tok % GW
            cnt_r = (jnp.minimum(gr + hi_w, NR) - jnp.maximum(gr - lo_w, 0)).astype(F32)
            cnt_c = (jnp.minimum(gc + hi_w, GW) - jnp.maximum(gc - lo_w, 0)).astype(F32)
            mean = band_sum(band, acc / cnt_r) / cnt_c
            finish(mean, Lc + r0)
            return carry

        lax.fori_loop(0, L // T, tile, 0)

    for gi, w in enumerate(windows):
        pl.when(g == gi)(functools.partial(body, w))


def _pool(P, pool_w, pool_scale, col0, B, L, Lc):
    G, cg = pool_w.shape[0], pool_w.shape[1]
    assert G == len(POOL_WINDOWS) and ROW_TILE % GRID_W == 0 and L % ROW_TILE == 0 and Lc % ROW_TILE == 0
    assert col0 % cg == 0
    RB = L + Lc
    PAD = (max(POOL_WINDOWS) // 2) * GRID_W
    return pl.pallas_call(
        functools.partial(_pool_kernel, Lc=Lc, L=L, windows=POOL_WINDOWS),
        grid=(B, G),
        in_specs=[pl.BlockSpec((cg // LANE, RB, LANE), lambda b, g: (col0 // cg + g, b, 0)),
                  pl.BlockSpec((None, cg, cg), lambda b, g: (g, 0, 0)),
                  pl.BlockSpec((1, cg), lambda b, g: (0, g))],
        out_specs=pl.BlockSpec((RB, cg), lambda b, g: (b, g)),
        out_shape=jax.ShapeDtypeStruct((B * RB, G * cg), BF16),
        scratch_shapes=[pltpu.VMEM((L + 2 * PAD, cg), F32)],
        compiler_params=_cp(2), name="pool",
    )(P, pool_w, pool_scale.reshape(1, G * cg))


def _mlstm_prep_kernel(u_ref, wg_ref, gb_ref, cq_ref, at_ref, wgt_scr, *, H):
    T = MLSTM_T

    @pl.when(pl.program_id(0) == 0)
    def _():
        wgt_scr[...] = wg_ref[...].T.astype(BF16)

    G = jnp.dot(u_ref[...], wgt_scr[...], preferred_element_type=F32) + gb_ref[...]
    ls = jax.nn.log_sigmoid(G)
    ri = lax.broadcasted_iota(I32, (T, T), 0)
    ci = lax.broadcasted_iota(I32, (T, T), 1)
    hp = lax.Precision.HIGHEST
    bf = jnp.dot((ci <= ri).astype(F32), ls, precision=hp, preferred_element_type=F32)
    bb = jnp.dot((ci >= ri).astype(F32), ls, precision=hp, preferred_element_type=F32)
    lane = lax.broadcasted_iota(I32, (T, LANE), 1)
    row = lax.broadcasted_iota(I32, (T, LANE), 0)
    fwd_lane = lane < 2 * H
    bsel = jnp.where(fwd_lane, bf, bb)
    a = G - pltpu.roll(bsel, LANE - H, 1)
    pf = a
    pb = a
    k = 1
    while k < T:
        pf = jnp.maximum(pf, jnp.where(row >= k, pltpu.roll(pf, k, 0), -jnp.inf))
        pb = jnp.maximum(pb, jnp.where(row < T - k, pltpu.roll(pb, T - k, 0), -jnp.inf))
        k *= 2
    is_i = (lane < H) | ((lane >= 2 * H) & (lane < 3 * H))
    cq_ref[...] = jnp.where(is_i, jnp.where(fwd_lane, pf, pb), bsel)
    at_ref[...] = a.T


def _mlstm_prep(u, w_gate_nk, gate_b, H):
    R, D = u.shape
    T = MLSTM_T
    gb = jnp.pad(gate_b.reshape(1, 4 * H), ((0, 0), (0, LANE - 4 * H)))
    wg = jnp.pad(w_gate_nk, ((0, LANE - 4 * H), (0, 0)))
    return pl.pallas_call(
        functools.partial(_mlstm_prep_kernel, H=H),
        grid=(R // T,),
        in_specs=[pl.BlockSpec((T, D), lambda i: (i, 0)), pl.BlockSpec((LANE, D), lambda i: (0, 0)),
                  pl.BlockSpec((1, LANE), lambda i: (0, 0))],
        out_specs=[pl.BlockSpec((T, LANE), lambda i: (i, 0)), pl.BlockSpec((None, LANE, T), lambda i: (i, 0, 0))],
        out_shape=[jax.ShapeDtypeStruct((R, LANE), F32), jax.ShapeDtypeStruct((R // T, LANE, T), F32)],
        scratch_shapes=[pltpu.VMEM((D, LANE), BF16)],
        compiler_params=_cp(1), name="mlstm_prep",
    )(u, wg, gb)


def _mlstm_kernel(q_ref, kt_ref, v_ref, og_ref, cq_ref, at_ref, ng_ref, y_ref,
                  hs_scr, c_scr, n_scr, m_scr, *, Lc, L, H, lat_only):
    T = MLSTM_T
    hh = pl.program_id(1)
    nc = (Lc + L) // T
    ncc = Lc // T
    scale = q_ref.shape[1] ** -0.5

    def wide(ref, rows):
        return jnp.concatenate([ref[p, rows, :] for p in range(ref.shape[0])], axis=1)

    ri = lax.broadcasted_iota(I32, (T, T), 0)
    ci = lax.broadcasted_iota(I32, (T, T), 1)
    ones_b = jnp.ones((T, LANE), BF16)

    for d in range(2):
        c_scr[...] = jnp.zeros(c_scr.shape, F32)
        n_scr[...] = jnp.zeros(n_scr.shape, F32)
        m_scr[...] = jnp.zeros(m_scr.shape, F32)
        tri = (ci <= ri) if d == 0 else (ci >= ri)
        end = T - 1 if d == 0 else 0

        def chunk(step, carry, d=d, tri=tri, end=end):
            if d == 0:
                c = step
            else:
                c = jnp.where(step < ncc, ncc - 1 - step, nc - 1 - (step - ncc))
            r0 = pl.multiple_of(c * T, T)
            q = q_ref[pl.ds(r0, T), :]
            kt = kt_ref[c]
            v = wide(v_ref, pl.ds(r0, T))
            cq = cq_ref[pl.ds(r0, T), :]
            pm_col = cq[:, 2 * d:2 * d + 1]
            b_col = cq[:, 2 * d + 1:2 * d + 2]
            a_row = at_ref[c, pl.ds(d * 2 * H + hh, 1), :]
            m = m_scr[...]
            mx = jnp.maximum(m, pm_col)
            dec = jnp.exp(jnp.where(tri, a_row - mx, -jnp.inf)) * scale
            s = jnp.dot(q, kt, preferred_element_type=F32)
            sw32 = s * dec
            sw = sw32.astype(BF16)
            e_int = jnp.exp(m - mx) * scale
            num = jnp.dot(sw, v, preferred_element_type=F32) + e_int * jnp.dot(
                q, c_scr[...].astype(BF16), preferred_element_type=F32)
            den = (jnp.sum(sw32, axis=1, keepdims=True)
                   + e_int * jnp.dot(q, n_scr[...].astype(BF16), preferred_element_type=F32)[:, 0:1])
            hout = num / jnp.maximum(jnp.abs(den), jnp.exp(-(b_col + mx)))
            if d == 0:
                hs_scr[pl.ds(r0, T), :] = hout
            else:
                hs_scr[pl.ds(r0, T), :] = hs_scr[pl.ds(r0, T), :] + hout
            mxe = jnp.maximum(m, pm_col[end:end + 1, :])
            decay = jnp.exp(m - mxe)
            kw = (kt.astype(F32) * jnp.exp(a_row - mxe)).astype(BF16)
            c_scr[...] = decay * c_scr[...] + jnp.dot(kw, v, preferred_element_type=F32)
            n_scr[...] = decay * n_scr[...] + jnp.dot(kw, ones_b, preferred_element_type=F32)
            m_scr[...] = b_col[end:end + 1, :] + mxe
            return carry

        lax.fori_loop(0, nc, chunk, 0)

    ng = ng_ref[...]
    first = ncc if lat_only else 0

    def finish(c, carry):
        r0 = pl.multiple_of(c * T, T)
        hn = _ln_rows(hs_scr[pl.ds(r0, T), :]) * ng
        y = _sigmoid(wide(og_ref, pl.ds(r0, T)).astype(F32)) * hn
        y_ref[pl.ds(pl.multiple_of(r0 - first * T, T), T), :] = y.astype(BF16)
        return carry

    lax.fori_loop(first, nc, finish, 0)


def _mlstm(QKVO, u, w_gate_nk, gate_b, norm_g, B, L, Lc, H, dqk, dv, lat_only):
    T = MLSTM_T
    RB = L + Lc
    R = B * RB
    nc = RB // T
    assert dqk % LANE == 0 and dv % LANE == 0 and L % T == 0 and Lc % T == 0 and 4 * H <= LANE
    D = H * dv
    out_rows = L if lat_only else RB
    vp = dv // dqk
    assert QKVO.shape[2] == dqk and dv % dqk == 0 and (2 * H) % vp == 0
    voff = (2 * H) // vp
    ooff = voff + H
    cq, at = _mlstm_prep(u, w_gate_nk, gate_b, H)
    cols = cq[:, :4 * H].reshape(R, 4, H).transpose(2, 0, 1)
    kt = QKVO[H:2 * H].reshape(H, R // T, T, dqk).transpose(1, 0, 3, 2).reshape(R // T, H * dqk, T)
    return pl.pallas_call(
        functools.partial(_mlstm_kernel, Lc=Lc, L=L, H=H, lat_only=lat_only),
        grid=(B, H),
        in_specs=[pl.BlockSpec((None, RB, dqk), lambda b, h: (h, b, 0)),
                  pl.BlockSpec((nc, dqk, T), lambda b, h: (b, h, 0)),
                  pl.BlockSpec((vp, RB, dqk), lambda b, h: (voff + h, b, 0)),
                  pl.BlockSpec((vp, RB, dqk), lambda b, h: (ooff + h, b, 0)),
                  pl.BlockSpec((None, RB, 4), lambda b, h: (h, b, 0)),
                  pl.BlockSpec((nc, LANE, T), lambda b, h: (b, 0, 0)),
                  pl.BlockSpec((1, dv), lambda b, h: (0, h))],
        out_specs=pl.BlockSpec((out_rows, dv), lambda b, h: (b, h)),
        out_shape=jax.ShapeDtypeStruct((B * out_rows, D), BF16),
        scratch_shapes=[pltpu.VMEM((RB, dv), F32), pltpu.VMEM((dqk, dv), F32),
                        pltpu.VMEM((dqk, LANE), F32), pltpu.VMEM((1, 1), F32)],
        compiler_params=_cp(2, VMEM_LIMIT_BIG), name="mlstm",
    )(QKVO, kt, QKVO, QKVO, cols, at, norm_g.reshape(1, D))


def _ln_mid_kernel(*refs, alpha, n_exp, tpb):
    n_res = 1 if tpb is None else 2
    res_refs = refs[:n_res]
    y_ref, mod_ref, g_ref, b_ref, wr_ref, h_out, u_out, aff_out, ub_scr, inv = refs[n_res:]
    _stage_rows(inv, [mod_ref[2:3, :], g_ref[...], b_ref[...], 1.0 + mod_ref[4:5, :], mod_ref[3:4, :]])

    def go(residual):
        def blk(r0):
            rows16 = pl.ds(r0, LN_ROWS)
            y = y_ref[rows16, :].astype(F32)
            us = []
            for k, rows in enumerate(_halves(r0)):
                v = alpha * residual(rows) + inv[0] * y[k * SUBLANE:(k + 1) * SUBLANE]
                h1 = _ln_rows(v) * inv[1] + inv[2]
                h_out[rows, :] = h1
                us.append(h1 * inv[3] + inv[4])
            u2 = jnp.concatenate(us, axis=0)
            ub_scr[rows16, :] = u2.astype(BF16)
            u_out[rows16, :] = u2
        _row_blocks(ROW_TILE, blk)

    if tpb is None:
        go(lambda rows: res_refs[0][rows, :])
    else:
        is_ctx = (pl.program_id(0) % tpb) == 0
        pl.when(is_ctx)(lambda: go(lambda rows: _ln_rows(res_refs[1][rows, :])))
        pl.when(jnp.logical_not(is_ctx))(lambda: go(lambda rows: _ln_rows(res_refs[0][rows, :])))

    logits = jnp.dot(ub_scr[...], wr_ref[...], preferred_element_type=F32)
    lane = lax.broadcasted_iota(I32, logits.shape, 1)
    logits = jnp.where(lane < n_exp, logits, -jnp.inf)
    e = jnp.exp(logits - jnp.max(logits, axis=-1, keepdims=True))
    aff_out[...] = e / jnp.sum(e, axis=-1, keepdims=True)


def _ln_mid(res, y, mods, layer, ln_g, ln_b, w_router, alpha, B, L, Lc, lat_only):
    raw = isinstance(res, tuple)
    D = y.shape[-1]
    E = w_router.shape[1]
    tpb = (L + Lc) // ROW_TILE
    lt = L // ROW_TILE
    wr = jnp.pad(w_router, ((0, 0), (0, LANE - E))).astype(BF16)
    if lat_only:
        assert not raw
        n_tiles = B * lt
        res_specs = [pl.BlockSpec((ROW_TILE, D), lambda i: (i + 1 + i // lt, 0))]
        mod_map = lambda i: (layer, i // lt, 0, 0)
    else:
        n_tiles = B * tpb
        mod_map = lambda i: (layer, jnp.where(i % tpb == 0, B, i // tpb), 0, 0)
        if raw:
            res_specs = [pl.BlockSpec((ROW_TILE, D), lambda i: ((i // tpb) * lt + jnp.maximum(i % tpb - 1, 0), 0)),
                         pl.BlockSpec((ROW_TILE, D), lambda i: (i // tpb, 0))]
        else:
            res_specs = [pl.BlockSpec((ROW_TILE, D), lambda i: (i, 0))]
    rows = n_tiles * ROW_TILE
    res_args = res if raw else (res,)
    return pl.pallas_call(
        functools.partial(_ln_mid_kernel, alpha=alpha, n_exp=E, tpb=tpb if raw else None),
        grid=(n_tiles,),
        in_specs=res_specs
        + [pl.BlockSpec((ROW_TILE, D), lambda i: (i, 0)),
                  pl.BlockSpec((None, None, 6, D), mod_map),
                  pl.BlockSpec((1, D), lambda i: (0, 0)),
                  pl.BlockSpec((1, D), lambda i: (0, 0)),
                  pl.BlockSpec((D, LANE), lambda i: (0, 0))],
        out_specs=[pl.BlockSpec((ROW_TILE, D), lambda i: (i, 0)),
                   pl.BlockSpec((ROW_TILE, D), lambda i: (i, 0)),
                   pl.BlockSpec((ROW_TILE, LANE), lambda i: (i, 0))],
        out_shape=[jax.ShapeDtypeStruct((rows, D), F32), jax.ShapeDtypeStruct((rows, D), F32),
                   jax.ShapeDtypeStruct((rows, LANE), F32)],
        scratch_shapes=[pltpu.VMEM((ROW_TILE, D), BF16), pltpu.VMEM((5, SUBLANE, D), F32)],
        compiler_params=_cp(1), name="ln_mid",
    )(*res_args, y, mods, ln_g.reshape(1, D), ln_b.reshape(1, D), wr)


def _moe_gu_kernel(rows_ref, u_hbm, wg_ref, wu_ref, o_ref, xg, xb, sem, *, ns, nf_static):
    e = pl.program_id(0)
    f = pl.program_id(1)
    ne = pl.num_programs(0)
    nf = pl.num_programs(1)
    per = ns // nf_static

    def row_copy(table_row, dst_row):
        r = rows_ref[table_row]
        return pltpu.make_async_copy(u_hbm.at[pl.ds(r, 1), :], xg.at[pl.ds(dst_row, 1), :], sem.at[0])

    def wait_all():
        pltpu.make_async_copy(u_hbm.at[pl.ds(0, ns), :], xg, sem.at[0]).wait()

    @pl.when((e == 0) & (f == 0))
    def _():
        def body(s, carry):
            row_copy(s, s).start()
            return carry
        lax.fori_loop(0, ns, body, 0, unroll=8)

    @pl.when(f == 0)
    def _():
        wait_all()
        xb[...] = xg[...].astype(BF16)

    first = (e + 1) * ns + f * per
    for s in range(per):
        row_copy(first + s, f * per + s).start(priority=s % 2)

    x = xb[...]
    hg = jnp.dot(x, wg_ref[...].astype(BF16), preferred_element_type=F32)
    hu = jnp.dot(x, wu_ref[...].astype(BF16), preferred_element_type=F32)
    o_ref[...] = (hg * _sigmoid(hg) * hu).astype(BF16)

    @pl.when((e == ne - 1) & (f == nf - 1))
    def _():
        wait_all()


def _moe_gu(rows, u2p, w_gu, layer):
    _, E, D, F2 = w_gu.shape
    F = F2 // 2
    ns = rows.shape[1]
    fc = min(MOE_FC, F)
    nf = F // fc
    assert ns % 8 == 0 and ns % nf == 0
    rows = jnp.concatenate([rows, rows[:1]], axis=0)
    grid_spec = pltpu.PrefetchScalarGridSpec(
        num_scalar_prefetch=1,
        grid=(E, nf),
        in_specs=[pl.BlockSpec(memory_space=pl.ANY),
                  pl.BlockSpec((None, None, D, fc), lambda e, f, rows: (layer, e, 0, f)),
                  pl.BlockSpec((None, None, D, fc), lambda e, f, rows: (layer, e, 0, nf + f))],
        out_specs=pl.BlockSpec((None, ns, fc), lambda e, f, rows: (e, 0, f)),
        scratch_shapes=[pltpu.VMEM((ns, D), F32), pltpu.VMEM((ns, D), BF16),
                        pltpu.SemaphoreType.DMA((1,))],
    )
    return pl.pallas_call(
        functools.partial(_moe_gu_kernel, ns=ns, nf_static=nf),
        grid_spec=grid_spec,
        out_shape=jax.ShapeDtypeStruct((E, ns, F), BF16),
        compiler_params=_cp(2), name="moe_gu",
    )(rows.reshape(-1), u2p, w_gu, w_gu)


def _moe_down_kernel(h_ref, w_ref, o_ref):
    o_ref[...] = jnp.dot(h_ref[...], w_ref[...].astype(BF16), preferred_element_type=F32).astype(BF16)


def _moe_down(hmid, w_down, layer):
    _, E, F, D = w_down.shape
    ns = hmid.shape[1]
    tn = min(MOE_TN, D)
    return pl.pallas_call(
        _moe_down_kernel,
        grid=(E, D // tn),
        in_specs=[pl.BlockSpec((None, ns, F), lambda e, j: (e, 0, 0)),
                  pl.BlockSpec((None, None, F, tn), lambda e, j: (layer, e, 0, j))],
        out_specs=pl.BlockSpec((None, ns, tn), lambda e, j: (e, 0, j)),
        out_shape=jax.ShapeDtypeStruct((E, ns, D), BF16),
        compiler_params=_cp(2), name="moe_down",
    )(hmid, w_down)


def _combine_kernel(a_ref, np_ref, y_hbm, pos_ref, aff_ref, av_ref, h_ref, mod_ref, modn_ref, g_ref, b_ref,
                    h_out, u_out, ybuf, acc, inv, sem, *, n_exp, ns, alpha, final):
    i = pl.program_id(0)
    n_tiles = pl.num_programs(0)
    slot = i % 2
    W = COMB_W
    pos = pos_ref[...]
    a_v = av_ref[...]
    gate = aff_ref[...].astype(BF16)
    sub = lax.broadcasted_iota(I32, (LANE, n_exp * W), 0)
    lan = lax.broadcasted_iota(I32, (LANE, n_exp * W), 1)
    expand = (lan // W == sub).astype(BF16)
    kk = (lax.broadcasted_iota(I32, (1, n_exp * W), 1) % W).astype(F32)

    def strip(tile, p, e, sl):
        w0 = pl.multiple_of(jnp.minimum(a_ref[tile * n_exp + e] + W * p, ns - W), COMB_ALIGN)
        return pltpu.make_async_copy(y_hbm.at[e, pl.ds(w0, W), :], ybuf.at[sl, pl.ds(e * W, W), :], sem.at[sl])

    def fetch(tile, p, sl):
        for e in range(n_exp):
            strip(tile, p, e, sl).start()

    def land(tile, p, sl):
        for e in range(n_exp):
            strip(tile, p, e, sl).wait()

    gate_x = jnp.dot(gate, expand, preferred_element_type=F32)

    def selector(p):
        start = a_v + W * p
        valid = (pos >= start) & (pos < start + W)
        rel = jnp.where(valid, pos - jnp.minimum(start, ns - W), -1).astype(F32).astype(BF16)
        relx = jnp.dot(rel, expand, preferred_element_type=F32)
        return jnp.where(relx == kk, gate_x, 0.0).astype(BF16)

    @pl.when(i == 0)
    def _():
        fetch(0, 0, 0)

    @pl.when(i + 1 < n_tiles)
    def _():
        fetch(i + 1, 0, 1 - slot)

    sel0 = selector(0)
    land(i, 0, slot)
    acc[...] = jnp.dot(sel0, ybuf[slot], preferred_element_type=F32)

    def extra_pass(p, carry):
        fetch(i, p, slot)
        selp = selector(p)
        land(i, p, slot)
        acc[...] += jnp.dot(selp, ybuf[slot], preferred_element_type=F32)
        return carry

    lax.fori_loop(1, np_ref[i], extra_pass, 0)

    vecs = [mod_ref[5:6, :], g_ref[...], b_ref[...]]
    if not final:
        vecs += [1.0 + modn_ref[1:2, :], modn_ref[0:1, :]]
    _stage_rows(inv, vecs)

    def blk(r0):
        us = []
        for rows in _halves(r0):
            v = alpha * h_ref[rows, :] + inv[0] * acc[rows, :]
            h2 = _ln_rows(v) * inv[1] + inv[2]
            h_out[rows, :] = h2
            if not final:
                us.append(h2 * inv[3] + inv[4])
        if not final:
            u_out[pl.ds(r0, LN_ROWS), :] = jnp.concatenate(us, axis=0).astype(BF16)

    _row_blocks(ROW_TILE, blk)


def _combine(Y, pos, aff, a_tab, npass, h1, mods, layer, ln_g, ln_b, alpha, B, L, Lc, lat_only, final):
    E, ns, D = Y.shape
    tpb = (L + Lc) // ROW_TILE
    lt = L // ROW_TILE
    n_tiles = h1.shape[0] // ROW_TILE
    nxt = layer if final else layer + 1
    if lat_only:
        sel = lambda i: i // lt
    else:
        sel = lambda i: jnp.where(i % tpb == 0, B, i // tpb)
    a_pad = a_tab.reshape(n_tiles, 1, LANE)
    out_shape = [jax.ShapeDtypeStruct((n_tiles * ROW_TILE, D), F32)]
    out_specs = [pl.BlockSpec((ROW_TILE, D), lambda i, a, n: (i, 0))]
    if not final:
        out_shape.append(jax.ShapeDtypeStruct((n_tiles * ROW_TILE, D), BF16))
        out_specs.append(pl.BlockSpec((ROW_TILE, D), lambda i, a, n: (i, 0)))
    grid_spec = pltpu.PrefetchScalarGridSpec(
        num_scalar_prefetch=2,
        grid=(n_tiles,),
        in_specs=[pl.BlockSpec(memory_space=pl.ANY),
                  pl.BlockSpec((ROW_TILE, LANE), lambda i, a, n: (i, 0)),
                  pl.BlockSpec((ROW_TILE, LANE), lambda i, a, n: (i, 0)),
                  pl.BlockSpec((None, 1, LANE), lambda i, a, n: (i, 0, 0)),
                  pl.BlockSpec((ROW_TILE, D), lambda i, a, n: (i, 0)),
                  pl.BlockSpec((None, None, 6, D), lambda i, a, n: (layer, sel(i), 0, 0)),
                  pl.BlockSpec((None, None, 6, D), lambda i, a, n: (nxt, sel(i), 0, 0)),
                  pl.BlockSpec((1, D), lambda i, a, n: (0, 0)),
                  pl.BlockSpec((1, D), lambda i, a, n: (0, 0))],
        out_specs=out_specs,
        scratch_shapes=[pltpu.VMEM((2, E * COMB_W, D), BF16), pltpu.VMEM((ROW_TILE, D), F32),
                        pltpu.VMEM((3 if final else 5, SUBLANE, D), F32), pltpu.SemaphoreType.DMA((2,))],
    )

    def kern(*refs):
        if final:
            _combine_kernel(*refs[:12], None, *refs[12:], n_exp=E, ns=ns, alpha=alpha, final=True)
        else:
            _combine_kernel(*refs, n_exp=E, ns=ns, alpha=alpha, final=False)

    return pl.pallas_call(
        kern, grid_spec=grid_spec, out_shape=out_shape,
        compiler_params=_cp(1), name="moe_combine",
    )(a_tab[:, :E].reshape(-1), npass, Y, pos, aff, a_pad, h1, mods, mods, ln_g.reshape(1, D), ln_b.reshape(1, D))


def _route(aff, E, B, L, Lc, lat_only):
    RB = L if lat_only else L + Lc
    n_rows = B * RB
    a3 = aff[:, :E].reshape(B, RB, E)

    def pick(a, n, off, slot0):
        cap = CAPACITY_FACTOR * n // E
        at = jnp.swapaxes(a, 1, 2)
        top, idx = lax.top_k(at, cap)
        idx = jnp.sort(idx, axis=-1)
        rows = idx + off + (jnp.arange(B, dtype=I32) * RB)[:, None, None]
        thr = top[..., cap - 1:cap]
        gt = at > thr
        eq = at == thr
        need = cap - jnp.sum(gt, axis=-1, keepdims=True)
        sel = gt | (eq & (jnp.cumsum(eq, axis=-1) <= need))
        slot = jnp.cumsum(sel, axis=-1) - 1 + slot0 + (jnp.arange(B, dtype=I32) * cap)[:, None, None]
        pos = jnp.swapaxes(jnp.where(sel, slot, -1), 1, 2).astype(I32)
        return jnp.swapaxes(rows, 0, 1).reshape(E, B * cap), pos

    if lat_only:
        rows, pos = pick(a3, L, 0, 0)
    else:
        r_l, p_l = pick(a3[:, Lc:], L, Lc, 0)
        r_c, p_c = pick(a3[:, :Lc], Lc, 0, r_l.shape[1])
        rows = jnp.concatenate([r_l, r_c], axis=1)
        pos = jnp.concatenate([p_c, p_l], axis=1)
    ns = rows.shape[1]
    assert ns % COMB_ALIGN == 0 and ns >= COMB_W
    pos = jnp.pad(pos.reshape(n_rows, E), ((0, 0), (0, LANE - E)), constant_values=-1)
    pt = pos.reshape(n_rows // ROW_TILE, ROW_TILE, LANE)
    has = pt >= 0
    cnt = jnp.sum(has, axis=1)
    lo = jnp.min(jnp.where(has, pt, ns), axis=1)
    lo = jnp.where(cnt > 0, lo, 0)
    a_tab = (lo // COMB_ALIGN) * COMB_ALIGN
    span = lo + cnt - a_tab
    npass = jnp.maximum(jnp.max((span + COMB_W - 1) // COMB_W, axis=1), 1).astype(I32)
    return rows.astype(I32), pos, a_tab.astype(I32), npass


def kernel(x, c, ctx, c_ctx, ada_w, ada_b, ln_g, ln_b, even_w_in, even_conv_w, even_conv_b, lru_gate_w,
           lru_gate_b, lru_lambda, pool_w, pool_scale, even_w_out, odd_w_in, odd_gate_b, odd_norm_g,
           odd_w_out, router_w, expert_w_gu, expert_w_down):
    B, L, D = x.shape
    Lc = ctx.shape[1]
    depth = ada_w.shape[0]
    E = router_w.shape[2]
    assert Lc == ROW_TILE and L % ROW_TILE == 0 and B + 1 <= 8 and depth >= 2
    alpha = (2 * depth) ** 0.25

    cond8 = jnp.zeros((8, D), F32).at[:B].set(c).at[B].set(c_ctx)
    mods = _adaln(cond8, ada_w, ada_b).reshape(depth, 8, 6, D)

    h = (x.reshape(B * L, D), ctx.reshape(B * Lc, D))
    u = _ln_in(*h, mods, B, L, Lc)

    for layer in range(depth):
        j = layer // 2
        last = layer == depth - 1
        if layer % 2 == 0:
            d_lru = lru_gate_w.shape[3] * lru_gate_w.shape[4]
            P = _matmul([u], even_w_in, j, even_w_in.shape[2], BF16, piece=LANE, name="even_w_in")
            mix_lru = _rglru(P, even_conv_w[j], even_conv_b[j], lru_gate_w[j], lru_gate_b[j], lru_lambda[j], B, L, Lc)
            mix_pool = _pool(P, pool_w[j], pool_scale[j], 2 * d_lru, B, L, Lc)
            if last:
                keep = lambda t: t.reshape(B, L + Lc, -1)[:, Lc:].reshape(B * L, -1)
                mix_lru, mix_pool = keep(mix_lru), keep(mix_pool)
            y = _matmul([mix_lru, mix_pool], even_w_out, j, D, BF16, name="even_w_out")
        else:
            H = odd_gate_b.shape[1] // 4
            dv = D // H
            dqk = dv // 2
            n_main = 2 * H * dqk + 2 * D
            w_in_t = jnp.swapaxes(odd_w_in, 1, 2)
            QKVO = _matmul([u], w_in_t, j, n_main, BF16, w_is_nk=True, piece=dqk, name="odd_w_in")
            hm = _mlstm(QKVO, u, w_in_t[j, n_main:, :], odd_gate_b[j], odd_norm_g[j], B, L, Lc, H, dqk, dv, last)
            y = _matmul([hm], odd_w_out, j, D, BF16, name="odd_w_out")

        h1, u2p, aff = _ln_mid(h, y, mods, layer, ln_g[layer, 0], ln_b[layer, 0], router_w[layer], alpha,
                               B, L, Lc, last)
        rows, pos, a_tab, npass = _route(aff, E, B, L, Lc, last)
        hmid = _moe_gu(rows, u2p, expert_w_gu, layer)
        Y = _moe_down(hmid, expert_w_down, layer)
        outs = _combine(Y, pos, aff, a_tab, npass, h1, mods, layer, ln_g[layer, 1], ln_b[layer, 1], alpha,
                        B, L, Lc, last, last)
        if last:
            return outs[0].reshape(B, L, D)
        h, u = outs
```
